```python
import jax, jax.numpy as jnp
from jax import lax
import numpy as np

D_MODEL = 1024
BATCH = 1
SEQ = 16384
DEPTH = 1
DEC_BATCH = 128
DEC_SEQ = 8
PAST_LEN = 8192
PAGE_SIZE = 128

N_HEADS = 8
HEAD_DIM = 64
ATTN_WIDTH = N_HEADS * HEAD_DIM
LRU_WIDTH = D_MODEL // 2
LRU_BLOCKS = 8
LRU_BLOCK_DIM = LRU_WIDTH // LRU_BLOCKS
LRU_CONV = 4
LRU_C = 8.0
FFN_DIM = 3 * D_MODEL
FFN_CONV = 3
Q_BLOCK = 128
RMS_EPS = 1e-6
SB_BIAS_INIT = -6.0
IN_WIDTH = 3 * ATTN_WIDTH + LRU_WIDTH + 2 * D_MODEL

kernel_name = "hybrid_stickbreak_rglru_convffn_step"


def rms_norm(x, g):
    x32 = x.astype(jnp.float32)
    y = x32 * lax.rsqrt(jnp.mean(x32 * x32, axis=-1, keepdims=True) + RMS_EPS)
    return (y * g.astype(jnp.float32)).astype(x.dtype)


def causal_dwconv(x, buf, w, b):
    t = x.shape[1]
    width = w.shape[0]
    xp = jnp.concatenate([buf.astype(x.dtype), x], axis=1)
    y = b + xp[:, 0:t] * w[0]
    for j in range(1, width):
        y = y + xp[:, j:j + t] * w[j]
    return y, xp[:, t:]


def stick_breaking(q, k, v, qpos, kpos, bias):
    z = jnp.einsum("qhd,khd->hqk", q.astype(jnp.float32), k.astype(jnp.float32)) * (HEAD_DIM ** -0.5)
    z = z + bias.astype(jnp.float32)[:, None, None]
    mask = kpos[None, :] < qpos[:, None]
    log_keep = jnp.where(mask, jax.nn.log_sigmoid(-z), 0.0)
    after = lax.cumsum(log_keep, axis=2, reverse=True) - log_keep
    w = jnp.where(mask, jnp.exp(jax.nn.log_sigmoid(z) + after), 0.0)
    return jnp.einsum("hqk,khd->qhd", w, v.astype(jnp.float32))


def prompt_attend(q, k, v, bias):
    bsz, t = q.shape[0], q.shape[1]
    kpos = jnp.arange(t)

    def block(i):
        start = i * Q_BLOCK
        qb = lax.dynamic_slice_in_dim(q, start, Q_BLOCK, axis=1)
        qpos = start + jnp.arange(Q_BLOCK)
        return jax.vmap(lambda qq, kk, vv: stick_breaking(qq, kk, vv, qpos, kpos, bias))(qb, k, v)

    out = lax.map(block, jnp.arange(t // Q_BLOCK))
    return jnp.moveaxis(out, 0, 1).reshape(bsz, t, N_HEADS, HEAD_DIM)


def make_sample_attend(cache_k, cache_v, page_table, layer):
    def attend(q, k, v, bias):
        t = q.shape[1]
        past = page_table.shape[1] * PAGE_SIZE
        qpos = past + jnp.arange(t)
        kpos = jnp.arange(past + t)

        def one(args):
            qb, kb, vb, pages = args
            kp = cache_k[layer, pages].reshape(past, N_HEADS, HEAD_DIM)
            vp = cache_v[layer, pages].reshape(past, N_HEADS, HEAD_DIM)
            k_all = jnp.concatenate([kp, kb.astype(kp.dtype)], axis=0)
            v_all = jnp.concatenate([vp, vb.astype(vp.dtype)], axis=0)
            return stick_breaking(qb, k_all, v_all, qpos, kpos, bias)

        return lax.map(one, (q, k, v, page_table))
    return attend


def rg_lru(xc, h0, w_a, b_a, w_x, b_x, lam):
    bsz, t, _ = xc.shape
    x32 = xc.astype(jnp.float32)
    xb = x32.reshape(bsz, t, LRU_BLOCKS, LRU_BLOCK_DIM)
    r = jax.nn.sigmoid(jnp.einsum("btnd,nde->btne", xb, w_a) + b_a).reshape(bsz, t, LRU_WIDTH)
    i = jax.nn.sigmoid(jnp.einsum("btnd,nde->btne", xb, w_x) + b_x).reshape(bsz, t, LRU_WIDTH)
    log_a = LRU_C * r * jax.nn.log_sigmoid(lam.astype(jnp.float32))
    a = jnp.exp(log_a)
    b = jnp.sqrt(-jnp.expm1(2.0 * log_a)) * (i * x32)
    b = b.at[:, 0].add(a[:, 0] * h0.astype(jnp.float32))

    def combine(left, right):
        a_l, b_l = left
        a_r, b_r = right
        return a_l * a_r, a_r * b_l + b_r

    _, h = lax.associative_scan(combine, (a, b), axis=1)
    return h.astype(xc.dtype), h[:, -1].astype(h0.dtype)


def hybrid_layer(x, attend, lru_conv_buf, lru_h0, ffn_conv_buf, p):
    bsz, t, _ = x.shape
    xn = rms_norm(x, p["norm_mix_g"])
    proj = xn @ p["w_in"]
    q, k, v, xl, g = jnp.split(proj, [ATTN_WIDTH, 2 * ATTN_WIDTH, 3 * ATTN_WIDTH, 3 * ATTN_WIDTH + LRU_WIDTH], axis=-1)
    q = q.reshape(bsz, t, N_HEADS, HEAD_DIM)
    k = k.reshape(bsz, t, N_HEADS, HEAD_DIM)
    v = v.reshape(bsz, t, N_HEADS, HEAD_DIM)
    o_attn = attend(q, k, v, p["attn_logit_bias"]).reshape(bsz, t, ATTN_WIDTH).astype(x.dtype)
    xc, lru_conv_new = causal_dwconv(xl, lru_conv_buf, p["lru_conv_w"], p["lru_conv_b"])
    o_lru, h_last = rg_lru(xc, lru_h0, p["w_lru_a"], p["b_lru_a"], p["w_lru_x"], p["b_lru_x"], p["lru_lambda"])
    gates = jax.nn.sigmoid((g + p["b_gate"]).astype(jnp.float32)).astype(x.dtype)
    gate_a, gate_r = jnp.split(gates, 2, axis=-1)
    mixed = gate_a * (o_attn @ p["w_attn_o"]) + gate_r * (o_lru @ p["w_lru_o"])
    h = x + mixed @ p["w_out"]
    hn = rms_norm(h, p["norm_ffn_g"])
    gc, ffn_conv_new = causal_dwconv(hn @ p["w_ffn_gate"], ffn_conv_buf, p["ffn_conv_w"], p["ffn_conv_b"])
    h = h + (jax.nn.gelu(gc, approximate=True) * (hn @ p["w_ffn_up"])) @ p["w_ffn_down"]
    return h, k, v, lru_conv_new, h_last, ffn_conv_new


def setup_inputs(seed: int = 0) -> dict:
    key = jax.random.key(seed)
    ks = jax.random.split(key, 32)
    n_pages = PAST_LEN // PAGE_SIZE
    n_used = DEC_BATCH * n_pages
    n_pool = n_used + n_used // 4

    def nrm(k, shape, scale):
        return scale * jax.random.normal(k, shape, jnp.float32)

    page_table = jax.random.permutation(ks[0], n_pool)[:n_used].reshape(DEC_BATCH, n_pages).astype(jnp.int32)
    a0 = jax.random.uniform(ks[1], (DEPTH, LRU_WIDTH), jnp.float32, minval=0.9, maxval=0.999)
    p_root = a0 ** (1.0 / LRU_C)
    lru_lambda = jnp.log(p_root) - jnp.log1p(-p_root)
    return {
        "x_prompt": nrm(ks[2], (BATCH, SEQ, D_MODEL), 1.0),
        "x_sample": nrm(ks[3], (DEC_BATCH, DEC_SEQ, D_MODEL), 1.0),
        "cache_k": nrm(ks[4], (DEPTH, n_pool, PAGE_SIZE, N_HEADS, HEAD_DIM), 1.0),
        "cache_v": nrm(ks[5], (DEPTH, n_pool, PAGE_SIZE, N_HEADS, HEAD_DIM), 1.0),
        "page_table": page_table,
        "state_lru_conv": nrm(ks[6], (DEPTH, DEC_BATCH, LRU_CONV - 1, LRU_WIDTH), 1.0),
        "state_lru_h": nrm(ks[7], (DEPTH, DEC_BATCH, LRU_WIDTH), 0.5),
        "state_ffn_conv": nrm(ks[8], (DEPTH, DEC_BATCH, FFN_CONV - 1, FFN_DIM), 1.0),
        "norm_mix_g": 1.0 + nrm(ks[9], (DEPTH, D_MODEL), 0.02),
        "w_in": nrm(ks[10], (DEPTH, D_MODEL, IN_WIDTH), D_MODEL ** -0.5),
        "b_gate": nrm(ks[11], (DEPTH, 2 * D_MODEL), 0.02),
        "attn_logit_bias": SB_BIAS_INIT + nrm(ks[28], (DEPTH, N_HEADS), 0.3),
        "w_attn_o": nrm(ks[12], (DEPTH, ATTN_WIDTH, D_MODEL), ATTN_WIDTH ** -0.5),
        "w_lru_o": nrm(ks[13], (DEPTH, LRU_WIDTH, D_MODEL), LRU_WIDTH ** -0.5),
        "w_out": nrm(ks[14], (DEPTH, D_MODEL, D_MODEL), D_MODEL ** -0.5),
        "lru_conv_w": nrm(ks[15], (DEPTH, LRU_CONV, LRU_WIDTH), LRU_CONV ** -0.5),
        "lru_conv_b": nrm(ks[16], (DEPTH, LRU_WIDTH), 0.02),
        "w_lru_a": nrm(ks[17], (DEPTH, LRU_BLOCKS, LRU_BLOCK_DIM, LRU_BLOCK_DIM), LRU_BLOCK_DIM ** -0.5),
        "b_lru_a": nrm(ks[18], (DEPTH, LRU_BLOCKS, LRU_BLOCK_DIM), 0.02),
        "w_lru_x": nrm(ks[19], (DEPTH, LRU_BLOCKS, LRU_BLOCK_DIM, LRU_BLOCK_DIM), LRU_BLOCK_DIM ** -0.5),
        "b_lru_x": nrm(ks[20], (DEPTH, LRU_BLOCKS, LRU_BLOCK_DIM), 0.02),
        "lru_lambda": lru_lambda,
        "norm_ffn_g": 1.0 + nrm(ks[21], (DEPTH, D_MODEL), 0.02),
        "w_ffn_gate": nrm(ks[22], (DEPTH, D_MODEL, FFN_DIM), D_MODEL ** -0.5),
        "w_ffn_up": nrm(ks[23], (DEPTH, D_MODEL, FFN_DIM), D_MODEL ** -0.5),
        "ffn_conv_w": nrm(ks[24], (DEPTH, FFN_CONV, FFN_DIM), FFN_CONV ** -0.5),
        "ffn_conv_b": nrm(ks[25], (DEPTH, FFN_DIM), 0.02),
        "w_ffn_down": nrm(ks[26], (DEPTH, FFN_DIM, D_MODEL), FFN_DIM ** -0.5),
        "norm_final_g": 1.0 + nrm(ks[27], (D_MODEL,), 0.02),
    }


def reference(x_prompt, x_sample, cache_k, cache_v, page_table, state_lru_conv, state_lru_h, state_ffn_conv,
              norm_mix_g, w_in, b_gate, attn_logit_bias, w_attn_o, w_lru_o, w_out, lru_conv_w, lru_conv_b,
              w_lru_a, b_lru_a, w_lru_x, b_lru_x, lru_lambda, norm_ffn_g, w_ffn_gate, w_ffn_up,
              ffn_conv_w, ffn_conv_b, w_ffn_down, norm_final_g):
    bp = x_prompt.shape[0]
    hp, hs = x_prompt, x_sample
    kp_l, ks_l, vp_l, vs_l = [], [], [], []
    lcp_l, lcs_l, lhp_l, lhs_l, fcp_l, fcs_l = [], [], [], [], [], []
    for l in range(DEPTH):
        p = dict(norm_mix_g=norm_mix_g[l], w_in=w_in[l], b_gate=b_gate[l], attn_logit_bias=attn_logit_bias[l],
                 w_attn_o=w_attn_o[l], w_lru_o=w_lru_o[l], w_out=w_out[l], lru_conv_w=lru_conv_w[l],
                 lru_conv_b=lru_conv_b[l], w_lru_a=w_lru_a[l], b_lru_a=b_lru_a[l], w_lru_x=w_lru_x[l],
                 b_lru_x=b_lru_x[l], lru_lambda=lru_lambda[l], norm_ffn_g=norm_ffn_g[l],
                 w_ffn_gate=w_ffn_gate[l], w_ffn_up=w_ffn_up[l], ffn_conv_w=ffn_conv_w[l],
                 ffn_conv_b=ffn_conv_b[l], w_ffn_down=w_ffn_down[l])
        hp, kp, vp, lcp, lhp, fcp = hybrid_layer(
            hp, prompt_attend,
            jnp.zeros((bp, LRU_CONV - 1, LRU_WIDTH), x_prompt.dtype),
            jnp.zeros((bp, LRU_WIDTH), state_lru_h.dtype),
            jnp.zeros((bp, FFN_CONV - 1, FFN_DIM), x_prompt.dtype), p)
        hs, ksn, vsn, lcs, lhs, fcs = hybrid_layer(
            hs, make_sample_attend(cache_k, cache_v, page_table, l),
            state_lru_conv[l], state_lru_h[l], state_ffn_conv[l], p)
        kp_l.append(kp); vp_l.append(vp); ks_l.append(ksn); vs_l.append(vsn)
        lcp_l.append(lcp); lcs_l.append(lcs); lhp_l.append(lhp); lhs_l.append(lhs)
        fcp_l.append(fcp); fcs_l.append(fcs)
    y_prompt = rms_norm(hp, norm_final_g)
    y_sample = rms_norm(hs, norm_final_g)
    return (y_prompt, y_sample,
            jnp.stack(kp_l), jnp.stack(ks_l), jnp.stack(vp_l), jnp.stack(vs_l),
            jnp.stack(lcp_l), jnp.stack(lcs_l), jnp.stack(lhp_l), jnp.stack(lhs_l),
            jnp.stack(fcp_l), jnp.stack(fcs_l))
```

```python
import functools

import jax
import jax.numpy as jnp
from jax import lax
from jax.experimental import pallas as pl
from jax.experimental.pallas import tpu as pltpu

N_HEADS = 8
HEAD_DIM = 64
ATTN_WIDTH = N_HEADS * HEAD_DIM
LRU_C = 8.0
RMS_EPS = 1e-6
PAGE_SIZE = 128

F32 = jnp.float32
BF16 = jnp.bfloat16

VMEM_LIMIT_BYTES = 56 * 1024 * 1024
SUBLANES = 8


def _const_spec(shape):
    zeros = (0,) * len(shape)
    return pl.BlockSpec(shape, lambda *_: zeros, pipeline_mode=pl.Buffered(1))


def _log_sigmoid_neg(z):
    return -(jnp.maximum(z, 0.0) + jnp.log1p(jnp.exp(-jnp.abs(z))))


def _split_bf16(x):
    hi = x.astype(BF16)
    lo = (x - hi.astype(F32)).astype(BF16)
    return hi, lo


def _rms(x, g):
    ms = jnp.mean(x * x, axis=-1, keepdims=True)
    return (x * lax.rsqrt(ms + RMS_EPS)) * g


def _pre_kernel(x_ref, g_ref, win_ref, bgate_ref, cw_ref, cb_ref, wa_ref, ba_ref, wx_ref, bx_ref, lam_ref,
                conv0_ref, h0_ref,
                k_ref, v_ref, qh_ref, kh_ref, vh_ref, gates_ref, olru_ref, convn_ref, hlast_ref,
                xpad_ref, hcarry_ref):
    ti = pl.program_id(1)
    bb, tt, d = x_ref.shape
    rows = bb * tt
    width = xpad_ref.shape[-1]
    hist = cw_ref.shape[0] - 1

    @pl.when(ti == 0)
    def _():
        xpad_ref[:, SUBLANES - hist:SUBLANES, :] = conv0_ref[...]
        hcarry_ref[...] = h0_ref[...]

    x = x_ref[...].reshape(rows, d)
    xn = _rms(x, g_ref[...])
    proj = jnp.dot(xn.astype(BF16), win_ref[...], preferred_element_type=F32)

    a = ATTN_WIDTH
    k_ref[...] = proj[:, a:2 * a]
    v_ref[...] = proj[:, 2 * a:3 * a]
    scale = HEAD_DIM ** -0.5
    for h in range(N_HEADS):
        lo_, hi_ = h * HEAD_DIM, (h + 1) * HEAD_DIM
        qh_ref[h] = (proj[:, lo_:hi_] * scale).astype(qh_ref.dtype)
        kh_ref[h] = proj[:, a + lo_:a + hi_].astype(kh_ref.dtype)
        vh_ref[h] = proj[:, 2 * a + lo_:2 * a + hi_].astype(vh_ref.dtype)

    gates_ref[...] = jax.nn.sigmoid(proj[:, 3 * a + width:] + bgate_ref[...])

    xl = proj[:, 3 * a:3 * a + width].reshape(bb, tt, width)
    xpad_ref[:, SUBLANES:SUBLANES + tt, :] = xl
    xc = cb_ref[...].reshape(1, 1, width)
    for j in range(hist + 1):
        xc = xc + xpad_ref[:, SUBLANES - hist + j:SUBLANES - hist + j + tt, :] * cw_ref[j:j + 1, :].reshape(1, 1, width)
    tail = xpad_ref[:, SUBLANES - hist + tt:SUBLANES + tt, :]
    convn_ref[...] = tail
    xpad_ref[:, SUBLANES - hist:SUBLANES, :] = tail

    xc2 = xc.reshape(rows, width)
    xcb = xc2.astype(BF16)
    r = jax.nn.sigmoid(jnp.dot(xcb, wa_ref[...], preferred_element_type=F32) + ba_ref[...])
    i = jax.nn.sigmoid(jnp.dot(xcb, wx_ref[...], preferred_element_type=F32) + bx_ref[...])
    log_a = LRU_C * r * _log_sigmoid_neg(-lam_ref[...])
    av = jnp.exp(log_a)
    bv = jnp.sqrt(1.0 - av * av) * (i * xc2)

    A = av.reshape(bb, tt, width)
    B = bv.reshape(bb, tt, width)
    tidx = lax.broadcasted_iota(jnp.int32, (bb, tt, width), 1)
    s = 1
    while s < tt:
        a_sh = pltpu.roll(A, s, axis=1)
        b_sh = pltpu.roll(B, s, axis=1)
        valid = tidx >= s
        B = jnp.where(valid, A * b_sh + B, B)
        A = jnp.where(valid, A * a_sh, A)
        s *= 2
    hs = A * hcarry_ref[...].reshape(bb, 1, width) + B
    olru_ref[...] = hs.reshape(rows, width).astype(olru_ref.dtype)
    h_last = hs[:, tt - 1, :]
    hcarry_ref[...] = h_last
    hlast_ref[...] = h_last


def _pre_call(x, conv0, h0, p, *, bb, tt, head_dtype):
    b, t, d = x.shape
    nb, nt = b // bb, t // tt
    rows = bb * tt
    n = b * t
    width = conv0.shape[-1]
    hist = conv0.shape[1]
    in_width = p["w_in"].shape[1]
    gate_w = in_width - 3 * ATTN_WIDTH - width

    def row_idx(bi, ti):
        return (bi * nt + ti, 0)

    def head_idx(bi, ti):
        return (0, bi * nt + ti, 0)

    in_specs = [
        pl.BlockSpec((bb, tt, d), lambda bi, ti: (bi, ti, 0)),
        _const_spec((1, d)),
        _const_spec((d, in_width)),
        _const_spec((1, gate_w)),
        _const_spec((hist + 1, width)),
        _const_spec((1, width)),
        _const_spec((width, width)),
        _const_spec((1, width)),
        _const_spec((width, width)),
        _const_spec((1, width)),
        _const_spec((1, width)),
        pl.BlockSpec((bb, hist, width), lambda bi, ti: (bi, 0, 0)),
        pl.BlockSpec((bb, width), lambda bi, ti: (bi, 0)),
    ]
    out_shape = [
        jax.ShapeDtypeStruct((n, ATTN_WIDTH), F32),
        jax.ShapeDtypeStruct((n, ATTN_WIDTH), F32),
        jax.ShapeDtypeStruct((N_HEADS, n, HEAD_DIM), head_dtype),
        jax.ShapeDtypeStruct((N_HEADS, n, HEAD_DIM), head_dtype),
        jax.ShapeDtypeStruct((N_HEADS, n, HEAD_DIM), head_dtype),
        jax.ShapeDtypeStruct((n, gate_w), F32),
        jax.ShapeDtypeStruct((n, width), BF16),
        jax.ShapeDtypeStruct((b, hist, width), F32),
        jax.ShapeDtypeStruct((b, width), F32),
    ]
    out_specs = [
        pl.BlockSpec((rows, ATTN_WIDTH), row_idx),
        pl.BlockSpec((rows, ATTN_WIDTH), row_idx),
        pl.BlockSpec((N_HEADS, rows, HEAD_DIM), head_idx),
        pl.BlockSpec((N_HEADS, rows, HEAD_DIM), head_idx),
        pl.BlockSpec((N_HEADS, rows, HEAD_DIM), head_idx),
        pl.BlockSpec((rows, gate_w), row_idx),
        pl.BlockSpec((rows, width), row_idx),
        pl.BlockSpec((bb, hist, width), lambda bi, ti: (bi, 0, 0)),
        pl.BlockSpec((bb, width), lambda bi, ti: (bi, 0)),
    ]
    return pl.pallas_call(
        _pre_kernel,
        grid=(nb, nt),
        in_specs=in_specs,
        out_specs=out_specs,
        out_shape=out_shape,
        scratch_shapes=[pltpu.VMEM((bb, SUBLANES + tt, width), F32), pltpu.VMEM((bb, width), F32)],
        compiler_params=pltpu.CompilerParams(dimension_semantics=("arbitrary", "arbitrary"),
                                             vmem_limit_bytes=VMEM_LIMIT_BYTES),
        name="pre",
    )(x, p["norm_mix_g"], p["w_in"], p["b_gate"], p["lru_conv_w"], p["lru_conv_b"], p["w_lru_a"], p["b_lru_a"],
      p["w_lru_x"], p["b_lru_x"], p["lru_lambda"], conv0, h0)


def _sb_tile(z, tri, carry, mask):
    lk = _log_sigmoid_neg(z)
    if mask is not None:
        lk = jnp.where(mask, lk, 0.0)
    hi, lo = _split_bf16(lk)
    cin = (jnp.dot(hi, tri, preferred_element_type=F32) + jnp.dot(lo, tri, preferred_element_type=F32)) + carry
    w = jnp.exp(z + cin)
    if mask is not None:
        w = jnp.where(mask, w, 0.0)
    return w, cin[:, 0:1]


def _attn_prompt_kernel(bias_ref, q_ref, k_ref, v_ref, tri_ref, o_ref):
    h = pl.program_id(0)
    iq = pl.program_id(1)
    tq = q_ref.shape[1]
    tk = tri_ref.shape[0]
    q = q_ref[0]
    tri = tri_ref[...]
    bias = bias_ref[h]

    def tile(j, acc, carry, mask):
        start = pl.multiple_of(j * tk, tk)
        kt = k_ref[0, pl.ds(start, tk), :]
        vt = v_ref[0, pl.ds(start, tk), :]
        z = lax.dot_general(q, kt, (((1,), (1,)), ((), ())), preferred_element_type=F32) + bias
        w, carry = _sb_tile(z, tri, carry, mask)
        acc = acc + jnp.dot(w.astype(BF16), vt, preferred_element_type=F32)
        return acc, carry

    row = lax.broadcasted_iota(jnp.int32, (tq, tk), 0)
    col = lax.broadcasted_iota(jnp.int32, (tq, tk), 1)
    acc0 = jnp.zeros((tq, HEAD_DIM), F32)
    carry0 = jnp.zeros((tq, 1), F32)
    acc, carry = tile(iq, acc0, carry0, col < row)

    def body(jj, state):
        return tile(iq - 1 - jj, state[0], state[1], None)

    acc, carry = lax.fori_loop(0, iq, body, (acc, carry))
    o_ref[0] = acc.astype(o_ref.dtype)


def _attn_prompt_call(qh, kh, vh, bias, tri, *, tq):
    nh, t, hd = qh.shape
    return pl.pallas_call(
        _attn_prompt_kernel,
        grid=(nh, t // tq),
        in_specs=[
            pl.BlockSpec(memory_space=pltpu.SMEM),
            pl.BlockSpec((1, tq, hd), lambda h, i: (h, i, 0)),
            pl.BlockSpec((1, t, hd), lambda h, i: (h, 0, 0)),
            pl.BlockSpec((1, t, hd), lambda h, i: (h, 0, 0)),
            _const_spec(tri.shape),
        ],
        out_specs=pl.BlockSpec((1, tq, hd), lambda h, i: (h, i, 0)),
        out_shape=jax.ShapeDtypeStruct((nh, t, hd), BF16),
        compiler_params=pltpu.CompilerParams(dimension_semantics=("arbitrary", "arbitrary"),
                                             vmem_limit_bytes=VMEM_LIMIT_BYTES),
        name="attn_prompt",
    )(bias, qh, kh, vh, tri)


def _attn_sample_kernel(pt_ref, bias_ref, q_ref, kn_ref, vn_ref, kc_ref, vc_ref, tri_ref, o_ref,
                        kpad_ref, vpad_ref, acc_ref, carry_ref):
    del pt_ref
    p = pl.program_id(1)
    n_pages = pl.num_programs(1)
    t = q_ref.shape[1]
    tri = tri_ref[...]

    def process(load_k, load_v, mask):
        zs = []
        for h in range(N_HEADS):
            zh = lax.dot_general(q_ref[h], load_k(h), (((1,), (1,)), ((), ())), preferred_element_type=F32)
            zs.append(zh + bias_ref[h])
        z = jnp.concatenate(zs, axis=0)
        w, carry = _sb_tile(z, tri, carry_ref[...], mask)
        carry_ref[...] = carry
        for h in range(N_HEADS):
            wh = w[h * t:(h + 1) * t, :]
            acc_ref[h] = acc_ref[h] + jnp.dot(wh, load_v(h), preferred_element_type=F32)

    @pl.when(p == 0)
    def _():
        acc_ref[...] = jnp.zeros_like(acc_ref)
        carry_ref[...] = jnp.zeros_like(carry_ref)
        kpad_ref[...] = jnp.zeros_like(kpad_ref)
        vpad_ref[...] = jnp.zeros_like(vpad_ref)
        kpad_ref[:, 0:t, :] = kn_ref[...]
        vpad_ref[:, 0:t, :] = vn_ref[...]
        qpos = lax.broadcasted_iota(jnp.int32, (N_HEADS, t, PAGE_SIZE), 1).reshape(N_HEADS * t, PAGE_SIZE)
        kpos = lax.broadcasted_iota(jnp.int32, (N_HEADS * t, PAGE_SIZE), 1)
        process(lambda h: kpad_ref[h], lambda h: vpad_ref[h], kpos < qpos)

    process(lambda h: kc_ref[0, pl.ds(h, PAGE_SIZE, stride=N_HEADS), :],
            lambda h: vc_ref[0, pl.ds(h, PAGE_SIZE, stride=N_HEADS), :], None)

    @pl.when(p == n_pages - 1)
    def _():
        o_ref[...] = acc_ref[...]


def _attn_sample_call(qh, kh, vh, cache_k, cache_v, page_table, bias, tri, *, t):
    nh, n, hd = qh.shape
    b = n // t
    n_pages = page_table.shape[1]
    rows_per_page = PAGE_SIZE * N_HEADS

    def page_idx(bi, pi, pt):
        return (pt[bi, n_pages - 1 - pi], 0, 0)

    def seq_idx(bi, pi, pt):
        return (0, bi, 0)

    grid_spec = pltpu.PrefetchScalarGridSpec(
        num_scalar_prefetch=1,
        grid=(b, n_pages),
        in_specs=[
            pl.BlockSpec(memory_space=pltpu.SMEM),
            pl.BlockSpec((nh, t, hd), seq_idx),
            pl.BlockSpec((nh, t, hd), seq_idx),
            pl.BlockSpec((nh, t, hd), seq_idx),
            pl.BlockSpec((1, rows_per_page, hd), page_idx),
            pl.BlockSpec((1, rows_per_page, hd), page_idx),
            pl.BlockSpec(tri.shape, lambda bi, pi, pt: (0, 0)),
        ],
        out_specs=pl.BlockSpec((nh, t, hd), seq_idx),
        scratch_shapes=[
            pltpu.VMEM((nh, PAGE_SIZE, hd), F32),
            pltpu.VMEM((nh, PAGE_SIZE, hd), F32),
            pltpu.VMEM((nh, t, hd), F32),
            pltpu.VMEM((nh * t, 1), F32),
        ],
    )
    return pl.pallas_call(
        _attn_sample_kernel,
        grid_spec=grid_spec,
        out_shape=jax.ShapeDtypeStruct((nh, n, hd), F32),
        compiler_params=pltpu.CompilerParams(dimension_semantics=("arbitrary", "arbitrary"),
                                             vmem_limit_bytes=VMEM_LIMIT_BYTES),
        name="attn_sample",
    )(page_table, bias, qh, kh, vh, cache_k, cache_v, tri)


def _post_kernel(x_ref, oa_ref, ol_ref, gates_ref, wao_ref, wlo_ref, wout_ref, gffn_ref, wg_ref, wu_ref, wd_ref,
                 fcw_ref, fcb_ref, fconv0_ref, gfin_ref,
                 y_ref, fconvn_ref,
                 gpad_ref, *, chunk, final_norm):
    ti = pl.program_id(1)
    bb, tt, d = x_ref.shape
    rows = bb * tt
    ffn = wg_ref.shape[1]
    hist = fcw_ref.shape[0] - 1

    @pl.when(ti == 0)
    def _():
        gpad_ref[:, SUBLANES - hist:SUBLANES, :] = fconv0_ref[...]

    ao = jnp.dot(oa_ref[0].astype(BF16), wao_ref[0], preferred_element_type=F32)
    for h in range(1, N_HEADS):
        ao = ao + jnp.dot(oa_ref[h].astype(BF16), wao_ref[h], preferred_element_type=F32)
    lo = jnp.dot(ol_ref[...], wlo_ref[...], preferred_element_type=F32)
    mixed = gates_ref[:, 0:d] * ao + gates_ref[:, d:2 * d] * lo
    hres = x_ref[...].reshape(rows, d) + jnp.dot(mixed.astype(BF16), wout_ref[...], preferred_element_type=F32)
    hn = _rms(hres, gffn_ref[...]).astype(BF16)

    acc = jnp.zeros((rows, d), F32)
    for c in range(ffn // chunk):
        cs = slice(c * chunk, (c + 1) * chunk)
        gp = jnp.dot(hn, wg_ref[:, cs], preferred_element_type=F32).reshape(bb, tt, chunk)
        gpad_ref[:, SUBLANES:SUBLANES + tt, cs] = gp
        gc = fcb_ref[:, cs].reshape(1, 1, chunk)
        for j in range(hist + 1):
            gc = gc + (gpad_ref[:, SUBLANES - hist + j:SUBLANES - hist + j + tt, cs]
                       * fcw_ref[j:j + 1, cs].reshape(1, 1, chunk))
        tail = gpad_ref[:, SUBLANES - hist + tt:SUBLANES + tt, cs]
        fconvn_ref[:, :, cs] = tail
        gpad_ref[:, SUBLANES - hist:SUBLANES, cs] = tail
        up = jnp.dot(hn, wu_ref[:, cs], preferred_element_type=F32)
        act = jax.nn.gelu(gc.reshape(rows, chunk), approximate=True) * up
        acc = acc + jnp.dot(act.astype(BF16), wd_ref[cs, :], preferred_element_type=F32)

    out = hres + acc
    if final_norm:
        out = _rms(out, gfin_ref[...])
    y_ref[...] = out.reshape(bb, tt, d)


def _post_call(x, oa, ol, gates, fconv0, p, g_final, *, bb, tt, chunk, final_norm):
    b, t, d = x.shape
    nb, nt = b // bb, t // tt
    rows = bb * tt
    ffn = p["w_ffn_gate"].shape[1]
    hist = fconv0.shape[1]
    width = ol.shape[1]

    def row_idx(bi, ti):
        return (bi * nt + ti, 0)

    in_specs = [
        pl.BlockSpec((bb, tt, d), lambda bi, ti: (bi, ti, 0)),
        pl.BlockSpec((N_HEADS, rows, HEAD_DIM), lambda bi, ti: (0, bi * nt + ti, 0)),
        pl.BlockSpec((rows, width), row_idx),
        pl.BlockSpec((rows, 2 * d), row_idx),
        _const_spec((N_HEADS, HEAD_DIM, d)),
        _const_spec((width, d)),
        _const_spec((d, d)),
        _const_spec((1, d)),
        _const_spec((d, ffn)),
        _const_spec((d, ffn)),
        _const_spec((ffn, d)),
        _const_spec((hist + 1, ffn)),
        _const_spec((1, ffn)),
        pl.BlockSpec((bb, hist, ffn), lambda bi, ti: (bi, 0, 0)),
        _const_spec((1, d)),
    ]
    return pl.pallas_call(
        functools.partial(_post_kernel, chunk=chunk, final_norm=final_norm),
        grid=(nb, nt),
        in_specs=in_specs,
        out_specs=[pl.BlockSpec((bb, tt, d), lambda bi, ti: (bi, ti, 0)),
                   pl.BlockSpec((bb, hist, ffn), lambda bi, ti: (bi, 0, 0))],
        out_shape=[jax.ShapeDtypeStruct((b, t, d), F32), jax.ShapeDtypeStruct((b, hist, ffn), F32)],
        scratch_shapes=[pltpu.VMEM((bb, SUBLANES + tt, ffn), F32)],
        compiler_params=pltpu.CompilerParams(dimension_semantics=("arbitrary", "arbitrary"),
                                             vmem_limit_bytes=VMEM_LIMIT_BYTES),
        name="post",
    )(x, oa, ol, gates, p["w_attn_o"], p["w_lru_o"], p["w_out"], p["norm_ffn_g"], p["w_ffn_gate"], p["w_ffn_up"],
      p["w_ffn_down"], p["ffn_conv_w"], p["ffn_conv_b"], fconv0, g_final)


def _block_diag(w):
    n, d, e = w.shape
    eye = jnp.eye(n, dtype=w.dtype)
    return (w[:, :, None, :] * eye[:, None, :, None]).reshape(n * d, n * e)


def _tri(n):
    j = lax.broadcasted_iota(jnp.int32, (n, n), 0)
    s = lax.broadcasted_iota(jnp.int32, (n, n), 1)
    return (j >= s).astype(BF16)


def _layer_params(l, norm_mix_g, w_in, b_gate, w_attn_o, w_lru_o, w_out, lru_conv_w, lru_conv_b,
                  w_lru_a, b_lru_a, w_lru_x, b_lru_x, lru_lambda, norm_ffn_g, w_ffn_gate, w_ffn_up,
                  ffn_conv_w, ffn_conv_b, w_ffn_down):
    d = w_in.shape[1]
    row = lambda v: v[l].reshape(1, -1)
    return dict(
        norm_mix_g=row(norm_mix_g), w_in=w_in[l].astype(BF16), b_gate=row(b_gate),
        w_attn_o=w_attn_o[l].astype(BF16).reshape(N_HEADS, HEAD_DIM, d), w_lru_o=w_lru_o[l].astype(BF16),
        w_out=w_out[l].astype(BF16), lru_conv_w=lru_conv_w[l], lru_conv_b=row(lru_conv_b),
        w_lru_a=_block_diag(w_lru_a[l]).astype(BF16), b_lru_a=row(b_lru_a),
        w_lru_x=_block_diag(w_lru_x[l]).astype(BF16), b_lru_x=row(b_lru_x), lru_lambda=row(lru_lambda),
        norm_ffn_g=row(norm_ffn_g), w_ffn_gate=w_ffn_gate[l].astype(BF16), w_ffn_up=w_ffn_up[l].astype(BF16),
        ffn_conv_w=ffn_conv_w[l], ffn_conv_b=row(ffn_conv_b), w_ffn_down=w_ffn_down[l].astype(BF16))


def kernel(x_prompt, x_sample, cache_k, cache_v, page_table, state_lru_conv, state_lru_h, state_ffn_conv,
           norm_mix_g, w_in, b_gate, attn_logit_bias, w_attn_o, w_lru_o, w_out, lru_conv_w, lru_conv_b,
           w_lru_a, b_lru_a, w_lru_x, b_lru_x, lru_lambda, norm_ffn_g, w_ffn_gate, w_ffn_up,
           ffn_conv_w, ffn_conv_b, w_ffn_down, norm_final_g):
    depth = w_in.shape[0]
    bp, tp, d = x_prompt.shape
    bs, ts, _ = x_sample.shape
    width = state_lru_h.shape[-1]
    ffn = state_ffn_conv.shape[-1]
    n_pool = cache_k.shape[1]
    g_final = norm_final_g.reshape(1, d)

    tile_p = min(256, tp)
    seq_blk = min(32, bs)
    chunk = 512
    tri_p = _tri(tile_p)
    tri_s = _tri(PAGE_SIZE)

    hp, hs = x_prompt, x_sample
    outs = [[] for _ in range(10)]
    for l in range(depth):
        p = _layer_params(l, norm_mix_g, w_in, b_gate, w_attn_o, w_lru_o, w_out, lru_conv_w, lru_conv_b,
                          w_lru_a, b_lru_a, w_lru_x, b_lru_x, lru_lambda, norm_ffn_g, w_ffn_gate, w_ffn_up,
                          ffn_conv_w, ffn_conv_b, w_ffn_down)
        bias = attn_logit_bias[l]
        last = l == depth - 1

        kp, vp, qh, kh, vh, gates, olru, lcp, lhp = _pre_call(
            hp, jnp.zeros((bp, lru_conv_w.shape[1] - 1, width), F32), jnp.zeros((bp, width), F32), p,
            bb=1, tt=tile_p, head_dtype=BF16)
        assert bp == 1
        oa = _attn_prompt_call(qh, kh, vh, bias, tri_p, tq=tile_p)
        hp, fcp = _post_call(hp, oa, olru, gates, jnp.zeros((bp, ffn_conv_w.shape[1] - 1, ffn), F32), p, g_final,
                             bb=1, tt=tile_p, chunk=chunk, final_norm=last)

        ks, vs, qh, kh, vh, gates, olru, lcs, lhs = _pre_call(
            hs, state_lru_conv[l], state_lru_h[l], p, bb=seq_blk, tt=ts, head_dtype=F32)
        ck = cache_k[l].reshape(n_pool, PAGE_SIZE * N_HEADS, HEAD_DIM)
        cv = cache_v[l].reshape(n_pool, PAGE_SIZE * N_HEADS, HEAD_DIM)
        oa = _attn_sample_call(qh, kh, vh, ck, cv, page_table, bias, tri_s, t=ts)
        hs, fcs = _post_call(hs, oa, olru, gates, state_ffn_conv[l], p, g_final,
                             bb=seq_blk, tt=ts, chunk=chunk, final_norm=last)

        layer = (kp.reshape(bp, tp, N_HEADS, HEAD_DIM), ks.reshape(bs, ts, N_HEADS, HEAD_DIM),
                 vp.reshape(bp, tp, N_HEADS, HEAD_DIM), vs.reshape(bs, ts, N_HEADS, HEAD_DIM),
                 lcp, lcs, lhp, lhs, fcp, fcs)
        for acc, val in zip(outs, layer):
            acc.append(val)
    return (hp, hs) + tuple(jnp.stack(o) for o in outs)
```

```python
import functools

import jax
import jax.numpy as jnp
from jax import lax
from jax.experimental import pallas as pl
from jax.experimental.pallas import tpu as pltpu

N_HEADS = 8
HEAD_DIM = 64
ATTN_WIDTH = N_HEADS * HEAD_DIM
LRU_C = 8.0
RMS_EPS = 1e-6
PAGE_SIZE = 128

LOG2E = 1.4426950408889634

F32 = jnp.float32
BF16 = jnp.bfloat16

VMEM_LIMIT_BYTES = 56 * 1024 * 1024
SUBLANES = 8
LANES = 128
HEADS_PER_TILE = LANES // HEAD_DIM
HEAD_TILES = ATTN_WIDTH // LANES


def _const_spec(shape):
    zeros = (0,) * len(shape)
    return pl.BlockSpec(shape, lambda *_: zeros, pipeline_mode=pl.Buffered(1))


def _log_sigmoid(x):
    return jnp.minimum(x, 0.0) - jnp.log(1.0 + jnp.exp2(jnp.abs(x) * (-LOG2E)))


def _rms(x, g):
    ms = jnp.mean(x * x, axis=-1, keepdims=True)
    return (x * lax.rsqrt(ms + RMS_EPS)) * g


def _pre_kernel(x_ref, g_ref, win_ref, bgate_ref, cw_ref, cb_ref, wa_ref, ba_ref, wx_ref, bx_ref, lam_ref,
                conv0_ref, h0_ref, *refs, prompt_layout):
    qkv_refs = refs[:-6]
    gates_ref, olru_ref, convn_ref, hlast_ref, xpad_ref, hcarry_ref = refs[-6:]
    ti = pl.program_id(1)
    bb, tt, d = x_ref.shape
    rows = bb * tt
    width = xpad_ref.shape[-1]
    hist = cw_ref.shape[0] - 1

    @pl.when(ti == 0)
    def _():
        xpad_ref[:, SUBLANES - hist:SUBLANES, :] = conv0_ref[...]
        hcarry_ref[...] = h0_ref[...]

    x = x_ref[...].reshape(rows, d)
    xn = _rms(x, g_ref[...])
    proj = jnp.dot(xn.astype(BF16), win_ref[...], preferred_element_type=F32)

    a = ATTN_WIDTH
    qf = proj[:, 0:a] * -(HEAD_DIM ** -0.5)
    kf = proj[:, a:2 * a]
    vf = proj[:, 2 * a:3 * a]
    if prompt_layout:
        ktf_ref, vtf_ref, q_ref, kt_ref, v_ref = qkv_refs
        kt = kf.T
        ktf_ref[...] = kt
        vtf_ref[...] = vf.T
        kt_ref[0] = kt.astype(BF16).reshape(HEAD_TILES, LANES, rows)
        for g in range(HEAD_TILES):
            q_ref[g] = qf[:, g * LANES:(g + 1) * LANES].astype(BF16)
            v_ref[g] = vf[:, g * LANES:(g + 1) * LANES].astype(BF16)
    else:
        k_ref, v_ref, q_ref = qkv_refs
        k_ref[...] = kf
        v_ref[...] = vf
        q_ref[...] = qf

    gates_ref[...] = jax.nn.sigmoid(proj[:, 3 * a + width:] + bgate_ref[...])

    xl = proj[:, 3 * a:3 * a + width].reshape(bb, tt, width)
    xpad_ref[:, SUBLANES:SUBLANES + tt, :] = xl
    xc = cb_ref[...].reshape(1, 1, width)
    for j in range(hist + 1):
        xc = xc + xpad_ref[:, SUBLANES - hist + j:SUBLANES - hist + j + tt, :] * cw_ref[j:j + 1, :].reshape(1, 1, width)
    tail = xpad_ref[:, SUBLANES - hist + tt:SUBLANES + tt, :]
    convn_ref[...] = tail
    xpad_ref[:, SUBLANES - hist:SUBLANES, :] = tail

    xc2 = xc.reshape(rows, width)
    xcb = xc2.astype(BF16)
    r = jax.nn.sigmoid(jnp.dot(xcb, wa_ref[...], preferred_element_type=F32) + ba_ref[...])
    i = jax.nn.sigmoid(jnp.dot(xcb, wx_ref[...], preferred_element_type=F32) + bx_ref[...])
    log_a = LRU_C * r * _log_sigmoid(lam_ref[...])
    av = jnp.exp(log_a)
    bv = jnp.sqrt(1.0 - av * av) * (i * xc2)

    A = av.reshape(bb, tt, width)
    B = bv.reshape(bb, tt, width)
    tidx = lax.broadcasted_iota(jnp.int32, (bb, tt, width), 1)
    s = 1
    while s < tt:
        a_sh = pltpu.roll(A, s, axis=1)
        b_sh = pltpu.roll(B, s, axis=1)
        valid = tidx >= s
        B = jnp.where(valid, A * b_sh + B, B)
        A = jnp.where(valid, A * a_sh, A)
        s *= 2
    hs = A * hcarry_ref[...].reshape(bb, 1, width) + B
    olru_ref[...] = hs.reshape(rows, width).astype(olru_ref.dtype)
    h_last = hs[:, tt - 1, :]
    hcarry_ref[...] = h_last
    hlast_ref[...] = h_last


def _pre_call(x, conv0, h0, p, *, bb, tt, prompt_layout):
    b, t, d = x.shape
    nb, nt = b // bb, t // tt
    rows = bb * tt
    n = b * t
    width = conv0.shape[-1]
    hist = conv0.shape[1]
    in_width = p["w_in"].shape[1]
    gate_w = in_width - 3 * ATTN_WIDTH - width
    a = ATTN_WIDTH

    def row_idx(bi, ti):
        return (bi * nt + ti, 0)

    def col_idx(bi, ti):
        return (0, bi * nt + ti)

    def tile_idx(bi, ti):
        return (0, bi * nt + ti, 0)

    in_specs = [
        pl.BlockSpec((bb, tt, d), lambda bi, ti: (bi, ti, 0)),
        _const_spec((1, d)),
        _const_spec((d, in_width)),
        _const_spec((1, gate_w)),
        _const_spec((hist + 1, width)),
        _const_spec((1, width)),
        _const_spec((width, width)),
        _const_spec((1, width)),
        _const_spec((width, width)),
        _const_spec((1, width)),
        _const_spec((1, width)),
        pl.BlockSpec((bb, hist, width), lambda bi, ti: (bi, 0, 0)),
        pl.BlockSpec((bb, width), lambda bi, ti: (bi, 0)),
    ]
    if prompt_layout:
        assert b == 1 and bb == 1
        qkv_shape = [
            jax.ShapeDtypeStruct((a, n), F32),
            jax.ShapeDtypeStruct((a, n), F32),
            jax.ShapeDtypeStruct((HEAD_TILES, n, LANES), BF16),
            jax.ShapeDtypeStruct((nt, HEAD_TILES, LANES, tt), BF16),
            jax.ShapeDtypeStruct((HEAD_TILES, n, LANES), BF16),
        ]
        qkv_specs = [
            pl.BlockSpec((a, rows), col_idx),
            pl.BlockSpec((a, rows), col_idx),
            pl.BlockSpec((HEAD_TILES, rows, LANES), tile_idx),
            pl.BlockSpec((1, HEAD_TILES, LANES, tt), lambda bi, ti: (ti, 0, 0, 0)),
            pl.BlockSpec((HEAD_TILES, rows, LANES), tile_idx),
        ]
    else:
        qkv_shape = [jax.ShapeDtypeStruct((n, a), F32)] * 3
        qkv_specs = [pl.BlockSpec((rows, a), row_idx)] * 3
    out_shape = qkv_shape + [
        jax.ShapeDtypeStruct((n, gate_w), F32),
        jax.ShapeDtypeStruct((n, width), BF16),
        jax.ShapeDtypeStruct((b, hist, width), F32),
        jax.ShapeDtypeStruct((b, width), F32),
    ]
    out_specs = qkv_specs + [
        pl.BlockSpec((rows, gate_w), row_idx),
        pl.BlockSpec((rows, width), row_idx),
        pl.BlockSpec((bb, hist, width), lambda bi, ti: (bi, 0, 0)),
        pl.BlockSpec((bb, width), lambda bi, ti: (bi, 0)),
    ]
    return pl.pallas_call(
        functools.partial(_pre_kernel, prompt_layout=prompt_layout),
        grid=(nb, nt),
        in_specs=in_specs,
        out_specs=out_specs,
        out_shape=out_shape,
        scratch_shapes=[pltpu.VMEM((bb, SUBLANES + tt, width), F32), pltpu.VMEM((bb, width), F32)],
        compiler_params=pltpu.CompilerParams(dimension_semantics=("arbitrary", "arbitrary"),
                                             vmem_limit_bytes=VMEM_LIMIT_BYTES),
        name="pre",
    )(x, p["norm_mix_g"], p["w_in"], p["b_gate"], p["lru_conv_w"], p["lru_conv_b"], p["w_lru_a"], p["b_lru_a"],
      p["w_lru_x"], p["b_lru_x"], p["lru_lambda"], conv0, h0)


def _exp(x):
    return jnp.exp2(x * LOG2E)


def _sb_tile(nz, tri, carry, mask):
    lk = _log_sigmoid(nz)
    if mask is not None:
        lk = jnp.where(mask, lk, 0.0)
    cin = jnp.dot(lk.astype(BF16), tri, preferred_element_type=F32) + carry
    w = _exp(cin - nz)
    if mask is not None:
        w = jnp.where(mask, w, 0.0)
    return w, cin[:, 0:1]


def _attn_prompt_kernel(bias_ref, q_ref, kt_ref, v_ref, tri_ref, o_ref,
                        qs_ref, nz_buf, lk_buf, cin_buf, acc_ref, carry_ref):
    iq = pl.program_id(0)
    n_tiles, tq, _ = q_ref.shape
    tk = tri_ref.shape[0]
    rows = HEADS_PER_TILE * tq
    tri = tri_ref[...]
    first = lax.broadcasted_iota(jnp.int32, (tq, LANES), 1) < HEAD_DIM

    def neg_logits(g, j):
        nz = jnp.dot(qs_ref[g], kt_ref[j, g], preferred_element_type=F32)
        return jnp.concatenate([nz[e * tq:(e + 1) * tq] - bias_ref[HEADS_PER_TILE * g + e]
                                for e in range(HEADS_PER_TILE)], axis=0)

    def values(g, j):
        return v_ref[g, pl.ds(pl.multiple_of(j * tk, tk), tk), :]

    row = lax.broadcasted_iota(jnp.int32, (HEADS_PER_TILE, tq, tk), 1).reshape(rows, tk)
    col = lax.broadcasted_iota(jnp.int32, (rows, tk), 1)
    causal = col < row
    for g in range(n_tiles):
        qp = q_ref[g]
        zero = jnp.zeros_like(qp)
        qs_ref[g] = jnp.concatenate([jnp.where(first, qp, zero), jnp.where(first, zero, qp)], axis=0)
        w, carry = _sb_tile(neg_logits(g, iq), tri, jnp.zeros((rows, 1), F32), causal)
        acc_ref[g] = jnp.dot(w.astype(BF16), values(g, iq), preferred_element_type=F32)
        carry_ref[g] = carry

    def advance(u):
        g, jj = u
        wrap = jj + 1 >= iq
        return (jnp.where(wrap, g + 1, g), jnp.where(wrap, 0, jj + 1))

    def addr(u):
        g, jj = u
        return jnp.minimum(g, n_tiles - 1), jnp.maximum(iq - 1 - jj, 0)

    def iteration(slot, units, stages):
        ua, uc = units[0], units[2]
        ga, ja = addr(ua)
        gc, jc = addr(uc)
        nz = neg_logits(ga, ja)
        if "tri" in stages:
            cin_buf[1 - slot] = jnp.dot(lk_buf[1 - slot], tri, preferred_element_type=F32)
        if "w" in stages:
            carry = carry_ref[gc]
            cin = cin_buf[slot] + carry
            w = _exp(cin - nz_buf[slot])
            acc_ref[gc] = acc_ref[gc] + jnp.dot(w.astype(BF16), values(gc, jc), preferred_element_type=F32)
            carry_ref[gc] = cin[:, 0:1]
        nz_buf[slot] = nz
        lk_buf[slot] = _log_sigmoid(nz).astype(BF16)
        return (advance(ua), ua, units[1])

    def body(_, units):
        for slot in range(2):
            units = iteration(slot, units, ("tri", "w"))
        return units

    @pl.when(iq > 0)
    def _():
        zero = jnp.int32(0)
        units = ((zero, zero),) * 3
        units = iteration(0, units, ())
        units = iteration(1, units, ("tri",))
        lax.fori_loop(0, (n_tiles * iq) // 2, body, units)

    for g in range(n_tiles):
        acc = acc_ref[g]
        o_ref[g] = jnp.where(first, acc[0:tq], acc[tq:rows]).astype(o_ref.dtype)


def _attn_prompt_call(q, kt, v, bias, tri, *, tq):
    n_tiles, t, _ = q.shape
    nt, _, _, tk = kt.shape
    assert tq == tk and n_tiles % 2 == 0
    rows = HEADS_PER_TILE * tq
    return pl.pallas_call(
        _attn_prompt_kernel,
        grid=(t // tq,),
        in_specs=[
            pl.BlockSpec(memory_space=pltpu.SMEM),
            pl.BlockSpec((n_tiles, tq, LANES), lambda i: (0, i, 0)),
            _const_spec(kt.shape),
            _const_spec(v.shape),
            _const_spec(tri.shape),
        ],
        out_specs=pl.BlockSpec((n_tiles, tq, LANES), lambda i: (0, i, 0)),
        out_shape=jax.ShapeDtypeStruct((n_tiles, t, LANES), BF16),
        scratch_shapes=[
            pltpu.VMEM((n_tiles, rows, LANES), BF16),
            pltpu.VMEM((2, rows, tk), F32),
            pltpu.VMEM((2, rows, tk), BF16),
            pltpu.VMEM((2, rows, tk), F32),
            pltpu.VMEM((n_tiles, rows, LANES), F32),
            pltpu.VMEM((n_tiles, rows, 1), F32),
        ],
        compiler_params=pltpu.CompilerParams(dimension_semantics=("arbitrary",),
                                             vmem_limit_bytes=VMEM_LIMIT_BYTES),
        name="attn_prompt",
    )(bias, q, kt, v, tri)


def _attn_sample_kernel(pt_ref, bias_ref, q_ref, kn_ref, vn_ref, *rest, pages):
    del pt_ref
    kc_refs, vc_refs = rest[:pages], rest[pages:2 * pages]
    tri_ref, o_ref, qbd_ref, bcol_ref, kpad_ref, vpad_ref, acc_ref, carry_ref = rest[2 * pages:]
    p = pl.program_id(1)
    t, a = q_ref.shape
    rows = N_HEADS * t
    tri = tri_ref[...]
    nt_dims = (((1,), (1,)), ((), ()))
    lane = lax.broadcasted_iota(jnp.int32, (t, a), 1)

    def own_lanes(h):
        return (lane >= h * HEAD_DIM) & (lane < (h + 1) * HEAD_DIM)

    def visit(tiles, mask):
        nzs = [zf() - bcol_ref[...] for zf, _ in tiles]
        lks = [_log_sigmoid(nz) for nz in nzs]
        if mask is not None:
            lks = [jnp.where(mask, lk, 0.0) for lk in lks]
        sums = [jnp.dot(lk.astype(BF16), tri, preferred_element_type=F32) for lk in lks]
        acc = acc_ref[...]
        carry = carry_ref[...]
        for nz, cin0, (_, of) in zip(nzs, sums, tiles):
            cin = cin0 + carry
            w = _exp(cin - nz)
            if mask is not None:
                w = jnp.where(mask, w, 0.0)
            acc = acc + of(w)
            carry = cin[:, 0:1]
        acc_ref[...] = acc
        carry_ref[...] = carry

    @pl.when(p == 0)
    def _():
        q = q_ref[...]
        for h in range(N_HEADS):
            qbd_ref[h * t:(h + 1) * t, :] = jnp.where(own_lanes(h), q, 0.0)
            bcol_ref[h * t:(h + 1) * t, :] = jnp.full((t, PAGE_SIZE), bias_ref[h], F32)
        acc_ref[...] = jnp.zeros_like(acc_ref)
        carry_ref[...] = jnp.zeros_like(carry_ref)
        kpad_ref[...] = jnp.zeros_like(kpad_ref)
        vpad_ref[...] = jnp.zeros_like(vpad_ref)
        kpad_ref[0:t, :] = kn_ref[...]
        vpad_ref[0:t, :] = vn_ref[...]
        qpos = lax.broadcasted_iota(jnp.int32, (N_HEADS, t, PAGE_SIZE), 1).reshape(rows, PAGE_SIZE)
        kpos = lax.broadcasted_iota(jnp.int32, (rows, PAGE_SIZE), 1)
        visit([(lambda: lax.dot_general(qbd_ref[...], kpad_ref[...], nt_dims, preferred_element_type=F32),
                lambda w: jnp.dot(w, vpad_ref[...], preferred_element_type=F32))], kpos < qpos)

    def page_tile(j):
        return (lambda: jnp.dot(qbd_ref[...], kc_refs[j][0], preferred_element_type=F32),
                lambda w: lax.dot_general(w, vc_refs[j][0], nt_dims, preferred_element_type=F32))

    visit([page_tile(j) for j in range(pages)], None)

    @pl.when(p == pl.num_programs(1) - 1)
    def _():
        acc = acc_ref[...]
        out = jnp.zeros((t, a), F32)
        for h in range(N_HEADS):
            out = jnp.where(own_lanes(h), acc[h * t:(h + 1) * t, :], out)
        o_ref[...] = out


def _attn_sample_call(q, k, v, cache_kt, cache_vt, page_table, bias, tri, *, t, pages):
    n, a = q.shape
    b = n // t
    n_pages = page_table.shape[1]
    assert n_pages % pages == 0
    rows = N_HEADS * t

    def page_idx(j):
        return lambda bi, pi, pt: (pt[bi, n_pages - 1 - (pi * pages + j)], 0, 0)

    def seq_idx(bi, pi, pt):
        return (bi, 0)

    page_spec = [pl.BlockSpec((1, a, PAGE_SIZE), page_idx(j)) for j in range(pages)]
    grid_spec = pltpu.PrefetchScalarGridSpec(
        num_scalar_prefetch=1,
        grid=(b, n_pages // pages),
        in_specs=[pl.BlockSpec(memory_space=pltpu.SMEM)] + [pl.BlockSpec((t, a), seq_idx)] * 3
        + page_spec + page_spec + [pl.BlockSpec(tri.shape, lambda bi, pi, pt: (0, 0))],
        out_specs=pl.BlockSpec((t, a), seq_idx),
        scratch_shapes=[
            pltpu.VMEM((rows, a), F32),
            pltpu.VMEM((rows, PAGE_SIZE), F32),
            pltpu.VMEM((PAGE_SIZE, a), F32),
            pltpu.VMEM((PAGE_SIZE, a), F32),
            pltpu.VMEM((rows, a), F32),
            pltpu.VMEM((rows, 1), F32),
        ],
    )
    return pl.pallas_call(
        functools.partial(_attn_sample_kernel, pages=pages),
        grid_spec=grid_spec,
        out_shape=jax.ShapeDtypeStruct((n, a), F32),
        compiler_params=pltpu.CompilerParams(dimension_semantics=("arbitrary", "arbitrary"),
                                             vmem_limit_bytes=VMEM_LIMIT_BYTES),
        name="attn_sample",
    )(page_table, bias, q, k, v, *([cache_kt] * pages), *([cache_vt] * pages), tri)


def _post_kernel(x_ref, oa_ref, ol_ref, gates_ref, wao_ref, wlo_ref, wout_ref, gffn_ref, wg_ref, wu_ref, wd_ref,
                 fcw_ref, fcb_ref, fconv0_ref, gfin_ref,
                 y_ref, fconvn_ref,
                 gpad_ref, *, chunk, final_norm):
    ti = pl.program_id(1)
    bb, tt, d = x_ref.shape
    rows = bb * tt
    ffn = wg_ref.shape[1]
    hist = fcw_ref.shape[0] - 1

    @pl.when(ti == 0)
    def _():
        gpad_ref[:, SUBLANES - hist:SUBLANES, :] = fconv0_ref[...]

    if len(oa_ref.shape) == 3:
        oa = jnp.concatenate([oa_ref[g] for g in range(oa_ref.shape[0])], axis=1)
    else:
        oa = oa_ref[...]
    ao = jnp.dot(oa.astype(BF16), wao_ref[...], preferred_element_type=F32)
    lo = jnp.dot(ol_ref[...], wlo_ref[...], preferred_element_type=F32)
    mixed = gates_ref[:, 0:d] * ao + gates_ref[:, d:2 * d] * lo
    hres = x_ref[...].reshape(rows, d) + jnp.dot(mixed.astype(BF16), wout_ref[...], preferred_element_type=F32)
    hn = _rms(hres, gffn_ref[...]).astype(BF16)

    acc = jnp.zeros((rows, d), F32)
    for c in range(ffn // chunk):
        cs = slice(c * chunk, (c + 1) * chunk)
        gp = jnp.dot(hn, wg_ref[:, cs], preferred_element_type=F32).reshape(bb, tt, chunk)
        gpad_ref[:, SUBLANES:SUBLANES + tt, cs] = gp
        gc = fcb_ref[:, cs].reshape(1, 1, chunk)
        for j in range(hist + 1):
            gc = gc + (gpad_ref[:, SUBLANES - hist + j:SUBLANES - hist + j + tt, cs]
                       * fcw_ref[j:j + 1, cs].reshape(1, 1, chunk))
        tail = gpad_ref[:, SUBLANES - hist + tt:SUBLANES + tt, cs]
        fconvn_ref[:, :, cs] = tail
        gpad_ref[:, SUBLANES - hist:SUBLANES, cs] = tail
        up = jnp.dot(hn, wu_ref[:, cs], preferred_element_type=F32)
        act = jax.nn.gelu(gc.reshape(rows, chunk), approximate=True) * up
        acc = acc + jnp.dot(act.astype(BF16), wd_ref[cs, :], preferred_element_type=F32)

    out = hres + acc
    if final_norm:
        out = _rms(out, gfin_ref[...])
    y_ref[...] = out.reshape(bb, tt, d)


def _post_call(x, oa, ol, gates, fconv0, p, g_final, *, bb, tt, chunk, final_norm):
    b, t, d = x.shape
    nb, nt = b // bb, t // tt
    rows = bb * tt
    ffn = p["w_ffn_gate"].shape[1]
    hist = fconv0.shape[1]
    width = ol.shape[1]

    def row_idx(bi, ti):
        return (bi * nt + ti, 0)

    if oa.ndim == 3:
        oa_spec = pl.BlockSpec((oa.shape[0], rows, oa.shape[2]), lambda bi, ti: (0, bi * nt + ti, 0))
    else:
        oa_spec = pl.BlockSpec((rows, oa.shape[1]), row_idx)
    in_specs = [
        pl.BlockSpec((bb, tt, d), lambda bi, ti: (bi, ti, 0)),
        oa_spec,
        pl.BlockSpec((rows, width), row_idx),
        pl.BlockSpec((rows, 2 * d), row_idx),
        _const_spec((ATTN_WIDTH, d)),
        _const_spec((width, d)),
        _const_spec((d, d)),
        _const_spec((1, d)),
        _const_spec((d, ffn)),
        _const_spec((d, ffn)),
        _const_spec((ffn, d)),
        _const_spec((hist + 1, ffn)),
        _const_spec((1, ffn)),
        pl.BlockSpec((bb, hist, ffn), lambda bi, ti: (bi, 0, 0)),
        _const_spec((1, d)),
    ]
    return pl.pallas_call(
        functools.partial(_post_kernel, chunk=chunk, final_norm=final_norm),
        grid=(nb, nt),
        in_specs=in_specs,
        out_specs=[pl.BlockSpec((bb, tt, d), lambda bi, ti: (bi, ti, 0)),
                   pl.BlockSpec((bb, hist, ffn), lambda bi, ti: (bi, 0, 0))],
        out_shape=[jax.ShapeDtypeStruct((b, t, d), F32), jax.ShapeDtypeStruct((b, hist, ffn), F32)],
        scratch_shapes=[pltpu.VMEM((bb, SUBLANES + tt, ffn), F32)],
        compiler_params=pltpu.CompilerParams(dimension_semantics=("arbitrary", "arbitrary"),
                                             vmem_limit_bytes=VMEM_LIMIT_BYTES),
        name="post",
    )(x, oa, ol, gates, p["w_attn_o"], p["w_lru_o"], p["w_out"], p["norm_ffn_g"], p["w_ffn_gate"], p["w_ffn_up"],
      p["w_ffn_down"], p["ffn_conv_w"], p["ffn_conv_b"], fconv0, g_final)


def _block_diag(w):
    n, d, e = w.shape
    eye = jnp.eye(n, dtype=w.dtype)
    return (w[:, :, None, :] * eye[:, None, :, None]).reshape(n * d, n * e)


def _tri(n):
    j = lax.broadcasted_iota(jnp.int32, (n, n), 0)
    s = lax.broadcasted_iota(jnp.int32, (n, n), 1)
    return (j >= s).astype(BF16)


def _layer_params(l, norm_mix_g, w_in, b_gate, w_attn_o, w_lru_o, w_out, lru_conv_w, lru_conv_b,
                  w_lru_a, b_lru_a, w_lru_x, b_lru_x, lru_lambda, norm_ffn_g, w_ffn_gate, w_ffn_up,
                  ffn_conv_w, ffn_conv_b, w_ffn_down):
    row = lambda v: v[l].reshape(1, -1)
    return dict(
        norm_mix_g=row(norm_mix_g), w_in=w_in[l].astype(BF16), b_gate=row(b_gate),
        w_attn_o=w_attn_o[l].astype(BF16), w_lru_o=w_lru_o[l].astype(BF16),
        w_out=w_out[l].astype(BF16), lru_conv_w=lru_conv_w[l], lru_conv_b=row(lru_conv_b),
        w_lru_a=_block_diag(w_lru_a[l]).astype(BF16), b_lru_a=row(b_lru_a),
        w_lru_x=_block_diag(w_lru_x[l]).astype(BF16), b_lru_x=row(b_lru_x), lru_lambda=row(lru_lambda),
        norm_ffn_g=row(norm_ffn_g), w_ffn_gate=w_ffn_gate[l].astype(BF16), w_ffn_up=w_ffn_up[l].astype(BF16),
        ffn_conv_w=ffn_conv_w[l], ffn_conv_b=row(ffn_conv_b), w_ffn_down=w_ffn_down[l].astype(BF16))


def kernel(x_prompt, x_sample, cache_k, cache_v, page_table, state_lru_conv, state_lru_h, state_ffn_conv,
           norm_mix_g, w_in, b_gate, attn_logit_bias, w_attn_o, w_lru_o, w_out, lru_conv_w, lru_conv_b,
           w_lru_a, b_lru_a, w_lru_x, b_lru_x, lru_lambda, norm_ffn_g, w_ffn_gate, w_ffn_up,
           ffn_conv_w, ffn_conv_b, w_ffn_down, norm_final_g):
    depth = w_in.shape[0]
    bp, tp, d = x_prompt.shape
    bs, ts, _ = x_sample.shape
    width = state_lru_h.shape[-1]
    ffn = state_ffn_conv.shape[-1]
    n_pool = cache_k.shape[1]
    n_pages = page_table.shape[1]
    g_final = norm_final_g.reshape(1, d)

    tile_p = min(256, tp)
    seq_blk = min(32, bs)
    chunk = 512
    pages = next(c for c in (8, 4, 2, 1) if n_pages % c == 0)
    tri_p = _tri(tile_p)
    tri_s = _tri(PAGE_SIZE)

    hp, hs = x_prompt, x_sample
    outs = [[] for _ in range(10)]
    for l in range(depth):
        p = _layer_params(l, norm_mix_g, w_in, b_gate, w_attn_o, w_lru_o, w_out, lru_conv_w, lru_conv_b,
                          w_lru_a, b_lru_a, w_lru_x, b_lru_x, lru_lambda, norm_ffn_g, w_ffn_gate, w_ffn_up,
                          ffn_conv_w, ffn_conv_b, w_ffn_down)
        bias = attn_logit_bias[l]
        last = l == depth - 1

        ktp, vtp, q, kt, v, gates, olru, lcp, lhp = _pre_call(
            hp, jnp.zeros((bp, lru_conv_w.shape[1] - 1, width), F32), jnp.zeros((bp, width), F32), p,
            bb=1, tt=tile_p, prompt_layout=True)
        oa = _attn_prompt_call(q, kt, v, bias, tri_p, tq=tile_p)
        hp, fcp = _post_call(hp, oa, olru, gates, jnp.zeros((bp, ffn_conv_w.shape[1] - 1, ffn), F32), p, g_final,
                             bb=1, tt=tile_p, chunk=chunk, final_norm=last)

        ks, vs, q, gates, olru, lcs, lhs = _pre_call(
            hs, state_lru_conv[l], state_lru_h[l], p, bb=seq_blk, tt=ts, prompt_layout=False)
        ckt = jnp.transpose(cache_k[l], (0, 2, 3, 1)).reshape(n_pool, ATTN_WIDTH, PAGE_SIZE)
        cvt = jnp.transpose(cache_v[l], (0, 2, 3, 1)).reshape(n_pool, ATTN_WIDTH, PAGE_SIZE)
        oa = _attn_sample_call(q, ks, vs, ckt, cvt, page_table, bias, tri_s, t=ts, pages=pages)
        hs, fcs = _post_call(hs, oa, olru, gates, state_ffn_conv[l], p, g_final,
                             bb=seq_blk, tt=ts, chunk=chunk, final_norm=last)

        from_t = lambda m: jnp.transpose(m.reshape(N_HEADS, HEAD_DIM, bp, tp), (2, 3, 0, 1))
        layer = (from_t(ktp), ks.reshape(bs, ts, N_HEADS, HEAD_DIM),
                 from_t(vtp), vs.reshape(bs, ts, N_HEADS, HEAD_DIM),
                 lcp, lcs, lhp, lhs, fcp, fcs)
        for acc, val in zip(outs, layer):
            acc.append(val)
    return (hp, hs) + tuple(jnp.stack(o) for o in outs)
```

```python
import functools

import jax
import jax.numpy as jnp
from jax import lax
from jax.experimental import pallas as pl
from jax.experimental.pallas import tpu as pltpu

N_HEADS = 8
HEAD_DIM = 64
ATTN_WIDTH = N_HEADS * HEAD_DIM
LRU_C = 8.0
RMS_EPS = 1e-6
PAGE_SIZE = 128

LOG2E = 1.4426950408889634

F32 = jnp.float32
BF16 = jnp.bfloat16

VMEM_LIMIT_BYTES = 56 * 1024 * 1024
SUBLANES = 8
LANES = 128
HEADS_PER_TILE = LANES // HEAD_DIM
HEAD_TILES = ATTN_WIDTH // LANES
UNITS_PER_TRIP = HEAD_TILES


def _const_spec(shape):
    zeros = (0,) * len(shape)
    return pl.BlockSpec(shape, lambda *_: zeros, pipeline_mode=pl.Buffered(1))


def _log_sigmoid(x):
    return jnp.minimum(x, 0.0) - jnp.log(1.0 + jnp.exp2(jnp.abs(x) * (-LOG2E)))


def _rms(x, g):
    ms = jnp.mean(x * x, axis=-1, keepdims=True)
    return (x * lax.rsqrt(ms + RMS_EPS)) * g


def _pre_kernel(x_ref, g_ref, win_ref, bgate_ref, cw_ref, cb_ref, wa_ref, ba_ref, wx_ref, bx_ref, lam_ref,
                conv0_ref, h0_ref, *refs, prompt_layout):
    qkv_refs = refs[:-6]
    gates_ref, olru_ref, convn_ref, hlast_ref, xpad_ref, hcarry_ref = refs[-6:]
    ti = pl.program_id(1)
    bb, tt, d = x_ref.shape
    rows = bb * tt
    width = xpad_ref.shape[-1]
    hist = cw_ref.shape[0] - 1

    @pl.when(ti == 0)
    def _():
        xpad_ref[:, SUBLANES - hist:SUBLANES, :] = conv0_ref[...]
        hcarry_ref[...] = h0_ref[...]

    x = x_ref[...].reshape(rows, d)
    xn = _rms(x, g_ref[...])
    proj = jnp.dot(xn.astype(BF16), win_ref[...], preferred_element_type=F32)

    a = ATTN_WIDTH
    qf = proj[:, 0:a] * -(HEAD_DIM ** -0.5)
    kf = proj[:, a:2 * a]
    vf = proj[:, 2 * a:3 * a]
    if prompt_layout:
        ktf_ref, vtf_ref, q_ref, kt_ref, v_ref = qkv_refs
        kt = kf.T
        ktf_ref[...] = kt
        vtf_ref[...] = vf.T
        kt_ref[0] = kt.astype(BF16).reshape(HEAD_TILES, LANES, rows)
        for g in range(HEAD_TILES):
            q_ref[g] = qf[:, g * LANES:(g + 1) * LANES].astype(BF16)
            v_ref[g] = vf[:, g * LANES:(g + 1) * LANES].astype(BF16)
    else:
        k_ref, v_ref, q_ref = qkv_refs
        k_ref[...] = kf
        v_ref[...] = vf
        q_ref[...] = qf

    gates_ref[...] = jax.nn.sigmoid(proj[:, 3 * a + width:] + bgate_ref[...])

    xl = proj[:, 3 * a:3 * a + width].reshape(bb, tt, width)
    xpad_ref[:, SUBLANES:SUBLANES + tt, :] = xl
    xc = cb_ref[...].reshape(1, 1, width)
    for j in range(hist + 1):
        xc = xc + xpad_ref[:, SUBLANES - hist + j:SUBLANES - hist + j + tt, :] * cw_ref[j:j + 1, :].reshape(1, 1, width)
    tail = xpad_ref[:, SUBLANES - hist + tt:SUBLANES + tt, :]
    convn_ref[...] = tail
    xpad_ref[:, SUBLANES - hist:SUBLANES, :] = tail

    xc2 = xc.reshape(rows, width)
    xcb = xc2.astype(BF16)
    r = jax.nn.sigmoid(jnp.dot(xcb, wa_ref[...], preferred_element_type=F32) + ba_ref[...])
    i = jax.nn.sigmoid(jnp.dot(xcb, wx_ref[...], preferred_element_type=F32) + bx_ref[...])
    log_a = LRU_C * r * _log_sigmoid(lam_ref[...])
    av = jnp.exp(log_a)
    bv = jnp.sqrt(1.0 - av * av) * (i * xc2)

    A = av.reshape(bb, tt, width)
    B = bv.reshape(bb, tt, width)
    tidx = lax.broadcasted_iota(jnp.int32, (bb, tt, width), 1)
    s = 1
    while s < tt:
        a_sh = pltpu.roll(A, s, axis=1)
        b_sh = pltpu.roll(B, s, axis=1)
        valid = tidx >= s
        B = jnp.where(valid, A * b_sh + B, B)
        A = jnp.where(valid, A * a_sh, A)
        s *= 2
    hs = A * hcarry_ref[...].reshape(bb, 1, width) + B
    olru_ref[...] = hs.reshape(rows, width).astype(olru_ref.dtype)
    h_last = hs[:, tt - 1, :]
    hcarry_ref[...] = h_last
    hlast_ref[...] = h_last


def _pre_call(x, conv0, h0, p, *, bb, tt, prompt_layout):
    b, t, d = x.shape
    nb, nt = b // bb, t // tt
    rows = bb * tt
    n = b * t
    width = conv0.shape[-1]
    hist = conv0.shape[1]
    in_width = p["w_in"].shape[1]
    gate_w = in_width - 3 * ATTN_WIDTH - width
    a = ATTN_WIDTH

    def row_idx(bi, ti):
        return (bi * nt + ti, 0)

    def col_idx(bi, ti):
        return (0, bi * nt + ti)

    def tile_idx(bi, ti):
        return (0, bi * nt + ti, 0)

    in_specs = [
        pl.BlockSpec((bb, tt, d), lambda bi, ti: (bi, ti, 0)),
        _const_spec((1, d)),
        _const_spec((d, in_width)),
        _const_spec((1, gate_w)),
        _const_spec((hist + 1, width)),
        _const_spec((1, width)),
        _const_spec((width, width)),
        _const_spec((1, width)),
        _const_spec((width, width)),
        _const_spec((1, width)),
        _const_spec((1, width)),
        pl.BlockSpec((bb, hist, width), lambda bi, ti: (bi, 0, 0)),
        pl.BlockSpec((bb, width), lambda bi, ti: (bi, 0)),
    ]
    if prompt_layout:
        assert b == 1 and bb == 1
        qkv_shape = [
            jax.ShapeDtypeStruct((a, n), F32),
            jax.ShapeDtypeStruct((a, n), F32),
            jax.ShapeDtypeStruct((HEAD_TILES, n, LANES), BF16),
            jax.ShapeDtypeStruct((nt, HEAD_TILES, LANES, tt), BF16),
            jax.ShapeDtypeStruct((HEAD_TILES, n, LANES), BF16),
        ]
        qkv_specs = [
            pl.BlockSpec((a, rows), col_idx),
            pl.BlockSpec((a, rows), col_idx),
            pl.BlockSpec((HEAD_TILES, rows, LANES), tile_idx),
            pl.BlockSpec((1, HEAD_TILES, LANES, tt), lambda bi, ti: (ti, 0, 0, 0)),
            pl.BlockSpec((HEAD_TILES, rows, LANES), tile_idx),
        ]
    else:
        qkv_shape = [jax.ShapeDtypeStruct((n, a), F32)] * 3
        qkv_specs = [pl.BlockSpec((rows, a), row_idx)] * 3
    out_shape = qkv_shape + [
        jax.ShapeDtypeStruct((n, gate_w), F32),
        jax.ShapeDtypeStruct((n, width), BF16),
        jax.ShapeDtypeStruct((b, hist, width), F32),
        jax.ShapeDtypeStruct((b, width), F32),
    ]
    out_specs = qkv_specs + [
        pl.BlockSpec((rows, gate_w), row_idx),
        pl.BlockSpec((rows, width), row_idx),
        pl.BlockSpec((bb, hist, width), lambda bi, ti: (bi, 0, 0)),
        pl.BlockSpec((bb, width), lambda bi, ti: (bi, 0)),
    ]
    return pl.pallas_call(
        functools.partial(_pre_kernel, prompt_layout=prompt_layout),
        grid=(nb, nt),
        in_specs=in_specs,
        out_specs=out_specs,
        out_shape=out_shape,
        scratch_shapes=[pltpu.VMEM((bb, SUBLANES + tt, width), F32), pltpu.VMEM((bb, width), F32)],
        compiler_params=pltpu.CompilerParams(dimension_semantics=("arbitrary", "arbitrary"),
                                             vmem_limit_bytes=VMEM_LIMIT_BYTES),
        name="pre",
    )(x, p["norm_mix_g"], p["w_in"], p["b_gate"], p["lru_conv_w"], p["lru_conv_b"], p["w_lru_a"], p["b_lru_a"],
      p["w_lru_x"], p["b_lru_x"], p["lru_lambda"], conv0, h0)


def _exp(x):
    return jnp.exp2(x * LOG2E)


def _sb_tile(nz, tri, carry, mask):
    lk = _log_sigmoid(nz)
    if mask is not None:
        lk = jnp.where(mask, lk, 0.0)
    cin = jnp.dot(lk.astype(BF16), tri, preferred_element_type=F32) + carry
    w = _exp(cin - nz)
    if mask is not None:
        w = jnp.where(mask, w, 0.0)
    return w, cin[:, 0:1]


def _attn_prompt_kernel(bias_ref, q_ref, kt_ref, v_ref, tri_ref, o_ref,
                        qs_ref, nz_buf, lk_buf, cin_buf, acc_ref, carry_ref):
    iq = pl.program_id(0)
    n_tiles, tq, _ = q_ref.shape
    tk = tri_ref.shape[0]
    rows = HEADS_PER_TILE * tq
    tri = tri_ref[...]
    first = lax.broadcasted_iota(jnp.int32, (tq, LANES), 1) < HEAD_DIM

    def neg_logits(g, j):
        nz = jnp.dot(qs_ref[g], kt_ref[j, g], preferred_element_type=F32)
        return jnp.concatenate([nz[e * tq:(e + 1) * tq] - bias_ref[HEADS_PER_TILE * g + e]
                                for e in range(HEADS_PER_TILE)], axis=0)

    def values(g, j):
        return v_ref[g, pl.ds(pl.multiple_of(j * tk, tk), tk), :]

    row = lax.broadcasted_iota(jnp.int32, (HEADS_PER_TILE, tq, tk), 1).reshape(rows, tk)
    col = lax.broadcasted_iota(jnp.int32, (rows, tk), 1)
    causal = col < row
    for g in range(n_tiles):
        qp = q_ref[g]
        zero = jnp.zeros_like(qp)
        qs_ref[g] = jnp.concatenate([jnp.where(first, qp, zero), jnp.where(first, zero, qp)], axis=0)
        w, carry = _sb_tile(neg_logits(g, iq), tri, jnp.zeros((rows, 1), F32), causal)
        acc_ref[g] = jnp.dot(w.astype(BF16), values(g, iq), preferred_element_type=F32)
        carry_ref[g] = carry

    def advance(u):
        g, jj = u
        wrap = jj + 1 >= iq
        return (jnp.where(wrap, g + 1, g), jnp.where(wrap, 0, jj + 1))

    def addr(u):
        g, jj = u
        return jnp.minimum(g, n_tiles - 1), jnp.maximum(iq - 1 - jj, 0)

    def iteration(slot, units, stages):
        ua, uc = units[0], units[2]
        ga, ja = addr(ua)
        gc, jc = addr(uc)
        nz = neg_logits(ga, ja)
        if "tri" in stages:
            cin_buf[1 - slot] = jnp.dot(lk_buf[1 - slot], tri, preferred_element_type=F32)
        if "w" in stages:
            carry = carry_ref[gc]
            cin = cin_buf[slot] + carry
            w = _exp(cin - nz_buf[slot])
            acc_ref[gc] = acc_ref[gc] + jnp.dot(w.astype(BF16), values(gc, jc), preferred_element_type=F32)
            carry_ref[gc] = cin[:, 0:1]
        nz_buf[slot] = nz
        lk_buf[slot] = _log_sigmoid(nz).astype(BF16)
        return (advance(ua), ua, units[1])

    def body(_, units):
        for k in range(UNITS_PER_TRIP):
            units = iteration(k % 2, units, ("tri", "w"))
        return units

    @pl.when(iq > 0)
    def _():
        zero = jnp.int32(0)
        units = ((zero, zero),) * 3
        units = iteration(0, units, ())
        units = iteration(1, units, ("tri",))
        lax.fori_loop(0, (n_tiles * iq) // UNITS_PER_TRIP, body, units)

    for g in range(n_tiles):
        acc = acc_ref[g]
        o_ref[g] = jnp.where(first, acc[0:tq], acc[tq:rows]).astype(o_ref.dtype)


def _attn_prompt_call(q, kt, v, bias, tri, *, tq):
    n_tiles, t, _ = q.shape
    nt, _, _, tk = kt.shape
    assert tq == tk and n_tiles % UNITS_PER_TRIP == 0 and UNITS_PER_TRIP % 2 == 0
    rows = HEADS_PER_TILE * tq
    return pl.pallas_call(
        _attn_prompt_kernel,
        grid=(t // tq,),
        in_specs=[
            pl.BlockSpec(memory_space=pltpu.SMEM),
            pl.BlockSpec((n_tiles, tq, LANES), lambda i: (0, i, 0)),
            _const_spec(kt.shape),
            _const_spec(v.shape),
            _const_spec(tri.shape),
        ],
        out_specs=pl.BlockSpec((n_tiles, tq, LANES), lambda i: (0, i, 0)),
        out_shape=jax.ShapeDtypeStruct((n_tiles, t, LANES), BF16),
        scratch_shapes=[
            pltpu.VMEM((n_tiles, rows, LANES), BF16),
            pltpu.VMEM((2, rows, tk), F32),
            pltpu.VMEM((2, rows, tk), BF16),
            pltpu.VMEM((2, rows, tk), F32),
            pltpu.VMEM((n_tiles, rows, LANES), F32),
            pltpu.VMEM((n_tiles, rows, 1), F32),
        ],
        compiler_params=pltpu.CompilerParams(dimension_semantics=("arbitrary",),
                                             vmem_limit_bytes=VMEM_LIMIT_BYTES),
        name="attn_prompt",
    )(bias, q, kt, v, tri)


K_PAGES, V_PAGES = 0, 1


def _attn_sample_kernel(pt_ref, bias_ref, q_ref, kn_ref, vn_ref, tri_ref, ck_hbm, cv_hbm, o_ref,
                        kbuf, vbuf, sems, qbd_ref, bcol_ref, kpad_ref, vpad_ref, nz_buf, lk_buf, cin_buf, nzw_buf,
                        acc_ref, carry_ref, *, pages):
    b = pl.program_id(0)
    t, a = q_ref.shape
    rows = N_HEADS * t
    n_pages = pt_ref.shape[1]
    n_groups = n_pages // pages
    tri = tri_ref[...]
    nt_dims = (((1,), (1,)), ((), ()))
    lane = lax.broadcasted_iota(jnp.int32, (t, a), 1)

    def own_lanes(h):
        return (lane >= h * HEAD_DIM) & (lane < (h + 1) * HEAD_DIM)

    def page_copies(kind, seq, grp, slot):
        hbm, buf = (ck_hbm, kbuf) if kind == K_PAGES else (cv_hbm, vbuf)
        return [pltpu.make_async_copy(hbm.at[pt_ref[seq, n_pages - 1 - (grp * pages + j)]], buf.at[slot, j],
                                      sems.at[kind, slot, j]) for j in range(pages)]

    def start(kind, seq, grp, slot):
        for copy in page_copies(kind, seq, grp, slot):
            copy.start()

    def wait(kind, seq, grp, slot):
        for copy in page_copies(kind, seq, grp, slot):
            copy.wait()

    def weights(nz, cin0, carry, mask):
        cin = cin0 + carry
        w = _exp(cin - nz)
        if mask is not None:
            w = jnp.where(mask, w, 0.0)
        return w, cin[:, 0:1]

    page_cols = [slice(j * PAGE_SIZE, (j + 1) * PAGE_SIZE) for j in range(pages)]

    def stage_z(slot):
        nzs = [jnp.dot(qbd_ref[...], kbuf[slot, j], preferred_element_type=F32) for j in range(pages)]
        for nz, cols in zip(nzs, page_cols):
            nz = nz - bcol_ref[...]
            nz_buf[slot, :, cols] = nz
            lk_buf[slot, :, cols] = _log_sigmoid(nz).astype(BF16)

    def stage_tri(slot):
        for cols in page_cols:
            cin_buf[slot, :, cols] = jnp.dot(lk_buf[slot, :, cols], tri, preferred_element_type=F32)
        nzw_buf[slot] = nz_buf[slot]

    def stage_w(slot):
        acc = acc_ref[...]
        carry = carry_ref[...]
        for j, cols in enumerate(page_cols):
            w, carry = weights(nzw_buf[slot, :, cols], cin_buf[slot, :, cols], carry, None)
            acc = acc + lax.dot_general(w, vbuf[slot, j], nt_dims, preferred_element_type=F32)
        acc_ref[...] = acc
        carry_ref[...] = carry

    @pl.when(b == 0)
    def _():
        start(K_PAGES, b, 0, 0)

    q = q_ref[...]
    for h in range(N_HEADS):
        qbd_ref[h * t:(h + 1) * t, :] = jnp.where(own_lanes(h), q, 0.0)
        bcol_ref[h * t:(h + 1) * t, :] = jnp.full((t, PAGE_SIZE), bias_ref[h], F32)
    kpad_ref[...] = jnp.zeros_like(kpad_ref)
    vpad_ref[...] = jnp.zeros_like(vpad_ref)
    kpad_ref[0:t, :] = kn_ref[...]
    vpad_ref[0:t, :] = vn_ref[...]
    qpos = lax.broadcasted_iota(jnp.int32, (N_HEADS, t, PAGE_SIZE), 1).reshape(rows, PAGE_SIZE)
    kpos = lax.broadcasted_iota(jnp.int32, (rows, PAGE_SIZE), 1)
    causal = kpos < qpos
    nz = lax.dot_general(qbd_ref[...], kpad_ref[...], nt_dims, preferred_element_type=F32) - bcol_ref[...]
    lk = jnp.where(causal, _log_sigmoid(nz), 0.0)
    w, carry = weights(nz, jnp.dot(lk.astype(BF16), tri, preferred_element_type=F32),
                       jnp.zeros((rows, 1), F32), causal)
    acc_ref[...] = jnp.dot(w, vpad_ref[...], preferred_element_type=F32)
    carry_ref[...] = carry

    def iteration(s, slot, stages):
        if "next_k" in stages:
            start(K_PAGES, b, s + 1, 1 - slot)
        if "tri" in stages:
            start(V_PAGES, b, s - 1, 1 - slot)
        if "z" in stages:
            wait(K_PAGES, b, s, slot)
        if "w" in stages:
            wait(V_PAGES, b, s - 2, slot)
        if "z" in stages:
            stage_z(slot)
        if "tri" in stages:
            stage_tri(1 - slot)
        if "w" in stages:
            stage_w(slot)

    iteration(0, 0, ("next_k", "z"))
    iteration(1, 1, ("next_k", "z", "tri"))

    def body(ii, carry_unused):
        for k in range(2):
            iteration(2 * ii + 2 + k, k, ("next_k", "z", "tri", "w"))
        return carry_unused

    lax.fori_loop(0, (n_groups - 4) // 2, body, 0)
    iteration(n_groups - 2, 0, ("next_k", "z", "tri", "w"))

    @pl.when(b + 1 < pl.num_programs(0))
    def _():
        start(K_PAGES, b + 1, 0, 0)

    iteration(n_groups - 1, 1, ("z", "tri", "w"))
    iteration(n_groups, 0, ("tri", "w"))
    iteration(n_groups + 1, 1, ("w",))

    acc = acc_ref[...]
    out = jnp.zeros((t, a), F32)
    for h in range(N_HEADS):
        out = jnp.where(own_lanes(h), acc[h * t:(h + 1) * t, :], out)
    o_ref[...] = out


def _attn_sample_call(q, k, v, cache_kt, cache_vt, page_table, bias, tri, *, t, pages):
    n, a = q.shape
    b = n // t
    n_pages = page_table.shape[1]
    assert n_pages % (2 * pages) == 0 and n_pages // pages >= 4
    rows = N_HEADS * t

    def seq_idx(bi, pt):
        return (bi, 0)

    grid_spec = pltpu.PrefetchScalarGridSpec(
        num_scalar_prefetch=1,
        grid=(b,),
        in_specs=[pl.BlockSpec(memory_space=pltpu.SMEM)] + [pl.BlockSpec((t, a), seq_idx)] * 3
        + [pl.BlockSpec(tri.shape, lambda bi, pt: (0, 0)),
           pl.BlockSpec(memory_space=pl.ANY), pl.BlockSpec(memory_space=pl.ANY)],
        out_specs=pl.BlockSpec((t, a), seq_idx),
        scratch_shapes=[
            pltpu.VMEM((2, pages, a, PAGE_SIZE), F32),
            pltpu.VMEM((2, pages, a, PAGE_SIZE), F32),
            pltpu.SemaphoreType.DMA((2, 2, pages)),
            pltpu.VMEM((rows, a), F32),
            pltpu.VMEM((rows, PAGE_SIZE), F32),
            pltpu.VMEM((PAGE_SIZE, a), F32),
            pltpu.VMEM((PAGE_SIZE, a), F32),
            pltpu.VMEM((2, rows, pages * PAGE_SIZE), F32),
            pltpu.VMEM((2, rows, pages * PAGE_SIZE), BF16),
            pltpu.VMEM((2, rows, pages * PAGE_SIZE), F32),
            pltpu.VMEM((2, rows, pages * PAGE_SIZE), F32),
            pltpu.VMEM((rows, a), F32),
            pltpu.VMEM((rows, 1), F32),
        ],
    )
    return pl.pallas_call(
        functools.partial(_attn_sample_kernel, pages=pages),
        grid_spec=grid_spec,
        out_shape=jax.ShapeDtypeStruct((n, a), F32),
        compiler_params=pltpu.CompilerParams(dimension_semantics=("arbitrary",),
                                             vmem_limit_bytes=VMEM_LIMIT_BYTES),
        name="attn_sample",
    )(page_table, bias, q, k, v, tri, cache_kt, cache_vt)


def _post_kernel(x_ref, oa_ref, ol_ref, gates_ref, wao_ref, wlo_ref, wout_ref, gffn_ref, wg_ref, wu_ref, wd_ref,
                 fcw_ref, fcb_ref, fconv0_ref, gfin_ref,
                 y_ref, fconvn_ref,
                 gpad_ref, *, chunk, final_norm):
    ti = pl.program_id(1)
    bb, tt, d = x_ref.shape
    rows = bb * tt
    ffn = wg_ref.shape[1]
    hist = fcw_ref.shape[0] - 1

    @pl.when(ti == 0)
    def _():
        gpad_ref[:, SUBLANES - hist:SUBLANES, :] = fconv0_ref[...]

    if len(oa_ref.shape) == 3:
        oa = jnp.concatenate([oa_ref[g] for g in range(oa_ref.shape[0])], axis=1)
    else:
        oa = oa_ref[...]
    ao = jnp.dot(oa.astype(BF16), wao_ref[...], preferred_element_type=F32)
    lo = jnp.dot(ol_ref[...], wlo_ref[...], preferred_element_type=F32)
    mixed = gates_ref[:, 0:d] * ao + gates_ref[:, d:2 * d] * lo
    hres = x_ref[...].reshape(rows, d) + jnp.dot(mixed.astype(BF16), wout_ref[...], preferred_element_type=F32)
    hn = _rms(hres, gffn_ref[...]).astype(BF16)

    acc = jnp.zeros((rows, d), F32)
    for c in range(ffn // chunk):
        cs = slice(c * chunk, (c + 1) * chunk)
        gp = jnp.dot(hn, wg_ref[:, cs], preferred_element_type=F32).reshape(bb, tt, chunk)
        gpad_ref[:, SUBLANES:SUBLANES + tt, cs] = gp
        gc = fcb_ref[:, cs].reshape(1, 1, chunk)
        for j in range(hist + 1):
            gc = gc + (gpad_ref[:, SUBLANES - hist + j:SUBLANES - hist + j + tt, cs]
                       * fcw_ref[j:j + 1, cs].reshape(1, 1, chunk))
        tail = gpad_ref[:, SUBLANES - hist + tt:SUBLANES + tt, cs]
        fconvn_ref[:, :, cs] = tail
        gpad_ref[:, SUBLANES - hist:SUBLANES, cs] = tail
        up = jnp.dot(hn, wu_ref[:, cs], preferred_element_type=F32)
        act = jax.nn.gelu(gc.reshape(rows, chunk), approximate=True) * up
        acc = acc + jnp.dot(act.astype(BF16), wd_ref[cs, :], preferred_element_type=F32)

    out = hres + acc
    if final_norm:
        out = _rms(out, gfin_ref[...])
    y_ref[...] = out.reshape(bb, tt, d)


def _post_call(x, oa, ol, gates, fconv0, p, g_final, *, bb, tt, chunk, final_norm):
    b, t, d = x.shape
    nb, nt = b // bb, t // tt
    rows = bb * tt
    ffn = p["w_ffn_gate"].shape[1]
    hist = fconv0.shape[1]
    width = ol.shape[1]

    def row_idx(bi, ti):
        return (bi * nt + ti, 0)

    if oa.ndim == 3:
        oa_spec = pl.BlockSpec((oa.shape[0], rows, oa.shape[2]), lambda bi, ti: (0, bi * nt + ti, 0))
    else:
        oa_spec = pl.BlockSpec((rows, oa.shape[1]), row_idx)
    in_specs = [
        pl.BlockSpec((bb, tt, d), lambda bi, ti: (bi, ti, 0)),
        oa_spec,
        pl.BlockSpec((rows, width), row_idx),
        pl.BlockSpec((rows, 2 * d), row_idx),
        _const_spec((ATTN_WIDTH, d)),
        _const_spec((width, d)),
        _const_spec((d, d)),
        _const_spec((1, d)),
        _const_spec((d, ffn)),
        _const_spec((d, ffn)),
        _const_spec((ffn, d)),
        _const_spec((hist + 1, ffn)),
        _const_spec((1, ffn)),
        pl.BlockSpec((bb, hist, ffn), lambda bi, ti: (bi, 0, 0)),
        _const_spec((1, d)),
    ]
    return pl.pallas_call(
        functools.partial(_post_kernel, chunk=chunk, final_norm=final_norm),
        grid=(nb, nt),
        in_specs=in_specs,
        out_specs=[pl.BlockSpec((bb, tt, d), lambda bi, ti: (bi, ti, 0)),
                   pl.BlockSpec((bb, hist, ffn), lambda bi, ti: (bi, 0, 0))],
        out_shape=[jax.ShapeDtypeStruct((b, t, d), F32), jax.ShapeDtypeStruct((b, hist, ffn), F32)],
        scratch_shapes=[pltpu.VMEM((bb, SUBLANES + tt, ffn), F32)],
        compiler_params=pltpu.CompilerParams(dimension_semantics=("arbitrary", "arbitrary"),
                                             vmem_limit_bytes=VMEM_LIMIT_BYTES),
        name="post",
    )(x, oa, ol, gates, p["w_attn_o"], p["w_lru_o"], p["w_out"], p["norm_ffn_g"], p["w_ffn_gate"], p["w_ffn_up"],
      p["w_ffn_down"], p["ffn_conv_w"], p["ffn_conv_b"], fconv0, g_final)


def _block_diag(w):
    n, d, e = w.shape
    eye = jnp.eye(n, dtype=w.dtype)
    return (w[:, :, None, :] * eye[:, None, :, None]).reshape(n * d, n * e)


def _tri(n):
    j = lax.broadcasted_iota(jnp.int32, (n, n), 0)
    s = lax.broadcasted_iota(jnp.int32, (n, n), 1)
    return (j >= s).astype(BF16)


def _layer_params(l, norm_mix_g, w_in, b_gate, w_attn_o, w_lru_o, w_out, lru_conv_w, lru_conv_b,
                  w_lru_a, b_lru_a, w_lru_x, b_lru_x, lru_lambda, norm_ffn_g, w_ffn_gate, w_ffn_up,
                  ffn_conv_w, ffn_conv_b, w_ffn_down):
    row = lambda v: v[l].reshape(1, -1)
    return dict(
        norm_mix_g=row(norm_mix_g), w_in=w_in[l].astype(BF16), b_gate=row(b_gate),
        w_attn_o=w_attn_o[l].astype(BF16), w_lru_o=w_lru_o[l].astype(BF16),
        w_out=w_out[l].astype(BF16), lru_conv_w=lru_conv_w[l], lru_conv_b=row(lru_conv_b),
        w_lru_a=_block_diag(w_lru_a[l]).astype(BF16), b_lru_a=row(b_lru_a),
        w_lru_x=_block_diag(w_lru_x[l]).astype(BF16), b_lru_x=row(b_lru_x), lru_lambda=row(lru_lambda),
        norm_ffn_g=row(norm_ffn_g), w_ffn_gate=w_ffn_gate[l].astype(BF16), w_ffn_up=w_ffn_up[l].astype(BF16),
        ffn_conv_w=ffn_conv_w[l], ffn_conv_b=row(ffn_conv_b), w_ffn_down=w_ffn_down[l].astype(BF16))


def kernel(x_prompt, x_sample, cache_k, cache_v, page_table, state_lru_conv, state_lru_h, state_ffn_conv,
           norm_mix_g, w_in, b_gate, attn_logit_bias, w_attn_o, w_lru_o, w_out, lru_conv_w, lru_conv_b,
           w_lru_a, b_lru_a, w_lru_x, b_lru_x, lru_lambda, norm_ffn_g, w_ffn_gate, w_ffn_up,
           ffn_conv_w, ffn_conv_b, w_ffn_down, norm_final_g):
    depth = w_in.shape[0]
    bp, tp, d = x_prompt.shape
    bs, ts, _ = x_sample.shape
    width = state_lru_h.shape[-1]
    ffn = state_ffn_conv.shape[-1]
    n_pool = cache_k.shape[1]
    n_pages = page_table.shape[1]
    g_final = norm_final_g.reshape(1, d)

    tile_p = min(256, tp)
    tile_post = min(512, tp)
    seq_blk = min(32, bs)
    chunk = 512
    pages = next(c for c in (8, 4, 2, 1) if n_pages % (2 * c) == 0 and n_pages // c >= 4)
    tri_p = _tri(tile_p)
    tri_s = _tri(PAGE_SIZE)

    hp, hs = x_prompt, x_sample
    outs = [[] for _ in range(10)]
    for l in range(depth):
        p = _layer_params(l, norm_mix_g, w_in, b_gate, w_attn_o, w_lru_o, w_out, lru_conv_w, lru_conv_b,
                          w_lru_a, b_lru_a, w_lru_x, b_lru_x, lru_lambda, norm_ffn_g, w_ffn_gate, w_ffn_up,
                          ffn_conv_w, ffn_conv_b, w_ffn_down)
        bias = attn_logit_bias[l]
        last = l == depth - 1

        ktp, vtp, q, kt, v, gates, olru, lcp, lhp = _pre_call(
            hp, jnp.zeros((bp, lru_conv_w.shape[1] - 1, width), F32), jnp.zeros((bp, width), F32), p,
            bb=1, tt=tile_p, prompt_layout=True)
        oa = _attn_prompt_call(q, kt, v, bias, tri_p, tq=tile_p)
        hp, fcp = _post_call(hp, oa, olru, gates, jnp.zeros((bp, ffn_conv_w.shape[1] - 1, ffn), F32), p, g_final,
                             bb=1, tt=tile_post, chunk=chunk, final_norm=last)

        ks, vs, q, gates, olru, lcs, lhs = _pre_call(
            hs, state_lru_conv[l], state_lru_h[l], p, bb=seq_blk, tt=ts, prompt_layout=False)
        ckt = jnp.transpose(cache_k[l], (0, 2, 3, 1)).reshape(n_pool, ATTN_WIDTH, PAGE_SIZE)
        cvt = jnp.transpose(cache_v[l], (0, 2, 3, 1)).reshape(n_pool, ATTN_WIDTH, PAGE_SIZE)
        oa = _attn_sample_call(q, ks, vs, ckt, cvt, page_table, bias, tri_s, t=ts, pages=pages)
        hs, fcs = _post_call(hs, oa, olru, gates, state_ffn_conv[l], p, g_final,
                             bb=seq_blk, tt=ts, chunk=chunk, final_norm=last)

        from_t = lambda m: jnp.transpose(m.reshape(N_HEADS, HEAD_DIM, bp, tp), (2, 3, 0, 1))
        layer = (from_t(ktp), ks.reshape(bs, ts, N_HEADS, HEAD_DIM),
                 from_t(vtp), vs.reshape(bs, ts, N_HEADS, HEAD_DIM),
                 lcp, lcs, lhp, lhs, fcp, fcs)
        for acc, val in zip(outs, layer):
            acc.append(val)
    return (hp, hs) + tuple(jnp.stack(o) for o in outs)
```

```python
import functools

import jax
import jax.numpy as jnp
from jax import lax
from jax.experimental import pallas as pl
from jax.experimental.pallas import tpu as pltpu

N_HEADS = 8
HEAD_DIM = 64
ATTN_WIDTH = N_HEADS * HEAD_DIM
LRU_C = 8.0
RMS_EPS = 1e-6
PAGE_SIZE = 128

LOG2E = 1.4426950408889634

F32 = jnp.float32
BF16 = jnp.bfloat16

VMEM_LIMIT_BYTES = 56 * 1024 * 1024
SUBLANES = 8
LANES = 128
HEADS_PER_TILE = LANES // HEAD_DIM
HEAD_TILES = ATTN_WIDTH // LANES
SHORT_TRIP = HEAD_TILES
LONG_TRIP = 2 * SHORT_TRIP


def _const_spec(shape):
    zeros = (0,) * len(shape)
    return pl.BlockSpec(shape, lambda *_: zeros, pipeline_mode=pl.Buffered(1))


def _log_sigmoid(x):
    return jnp.minimum(x, 0.0) - jnp.log(1.0 + jnp.exp2(jnp.abs(x) * (-LOG2E)))


def _log2_sigmoid(x2):
    neg_abs = pltpu.bitcast(pltpu.bitcast(x2, jnp.uint32) | jnp.uint32(0x80000000), F32)
    return jnp.minimum(x2, 0.0) - jnp.log2(1.0 + jnp.exp2(neg_abs))


def _rms(x, g):
    ms = jnp.mean(x * x, axis=-1, keepdims=True)
    return (x * lax.rsqrt(ms + RMS_EPS)) * g


def _pre_kernel(x_ref, g_ref, win_ref, bgate_ref, cw_ref, cb_ref, wa_ref, ba_ref, wx_ref, bx_ref, lam_ref,
                conv0_ref, h0_ref, *refs, prompt_layout):
    qkv_refs = refs[:-6]
    gates_ref, olru_ref, convn_ref, hlast_ref, xpad_ref, hcarry_ref = refs[-6:]
    ti = pl.program_id(1)
    bb, tt, d = x_ref.shape
    rows = bb * tt
    width = xpad_ref.shape[-1]
    hist = cw_ref.shape[0] - 1

    @pl.when(ti == 0)
    def _():
        xpad_ref[:, SUBLANES - hist:SUBLANES, :] = conv0_ref[...]
        hcarry_ref[...] = h0_ref[...]

    x = x_ref[...].reshape(rows, d)
    xn = _rms(x, g_ref[...])
    proj = jnp.dot(xn.astype(BF16), win_ref[...], preferred_element_type=F32)

    a = ATTN_WIDTH
    qf = proj[:, 0:a] * -(HEAD_DIM ** -0.5 * LOG2E)
    kf = proj[:, a:2 * a]
    vf = proj[:, 2 * a:3 * a]
    if prompt_layout:
        ktf_ref, vtf_ref, q_ref, kt_ref, v_ref = qkv_refs
        kt = kf.T
        ktf_ref[...] = kt
        vtf_ref[...] = vf.T
        kt_ref[0] = kt.astype(BF16).reshape(HEAD_TILES, LANES, rows)
        for g in range(HEAD_TILES):
            q_ref[g] = qf[:, g * LANES:(g + 1) * LANES].astype(BF16)
            v_ref[g] = vf[:, g * LANES:(g + 1) * LANES].astype(BF16)
    else:
        k_ref, v_ref, q_ref = qkv_refs
        k_ref[...] = kf
        v_ref[...] = vf
        q_ref[...] = qf

    gates_ref[...] = jax.nn.sigmoid(proj[:, 3 * a + width:] + bgate_ref[...])

    xl = proj[:, 3 * a:3 * a + width].reshape(bb, tt, width)
    xpad_ref[:, SUBLANES:SUBLANES + tt, :] = xl
    xc = cb_ref[...].reshape(1, 1, width)
    for j in range(hist + 1):
        xc = xc + xpad_ref[:, SUBLANES - hist + j:SUBLANES - hist + j + tt, :] * cw_ref[j:j + 1, :].reshape(1, 1, width)
    tail = xpad_ref[:, SUBLANES - hist + tt:SUBLANES + tt, :]
    convn_ref[...] = tail
    xpad_ref[:, SUBLANES - hist:SUBLANES, :] = tail

    xc2 = xc.reshape(rows, width)
    xcb = xc2.astype(BF16)
    r = jax.nn.sigmoid(jnp.dot(xcb, wa_ref[...], preferred_element_type=F32) + ba_ref[...])
    i = jax.nn.sigmoid(jnp.dot(xcb, wx_ref[...], preferred_element_type=F32) + bx_ref[...])
    log_a = LRU_C * r * _log_sigmoid(lam_ref[...])
    av = jnp.exp(log_a)
    bv = jnp.sqrt(1.0 - av * av) * (i * xc2)

    A = av.reshape(bb, tt, width)
    B = bv.reshape(bb, tt, width)
    tidx = lax.broadcasted_iota(jnp.int32, (bb, tt, width), 1)
    s = 1
    while s < tt:
        a_sh = pltpu.roll(A, s, axis=1)
        b_sh = pltpu.roll(B, s, axis=1)
        valid = tidx >= s
        B = jnp.where(valid, A * b_sh + B, B)
        A = jnp.where(valid, A * a_sh, A)
        s *= 2
    hs = A * hcarry_ref[...].reshape(bb, 1, width) + B
    olru_ref[...] = hs.reshape(rows, width).astype(olru_ref.dtype)
    h_last = hs[:, tt - 1, :]
    hcarry_ref[...] = h_last
    hlast_ref[...] = h_last


def _pre_call(x, conv0, h0, p, *, bb, tt, prompt_layout):
    b, t, d = x.shape
    nb, nt = b // bb, t // tt
    rows = bb * tt
    n = b * t
    width = conv0.shape[-1]
    hist = conv0.shape[1]
    in_width = p["w_in"].shape[1]
    gate_w = in_width - 3 * ATTN_WIDTH - width
    a = ATTN_WIDTH

    def row_idx(bi, ti):
        return (bi * nt + ti, 0)

    def col_idx(bi, ti):
        return (0, bi * nt + ti)

    def tile_idx(bi, ti):
        return (0, bi * nt + ti, 0)

    in_specs = [
        pl.BlockSpec((bb, tt, d), lambda bi, ti: (bi, ti, 0)),
        _const_spec((1, d)),
        _const_spec((d, in_width)),
        _const_spec((1, gate_w)),
        _const_spec((hist + 1, width)),
        _const_spec((1, width)),
        _const_spec((width, width)),
        _const_spec((1, width)),
        _const_spec((width, width)),
        _const_spec((1, width)),
        _const_spec((1, width)),
        pl.BlockSpec((bb, hist, width), lambda bi, ti: (bi, 0, 0)),
        pl.BlockSpec((bb, width), lambda bi, ti: (bi, 0)),
    ]
    if prompt_layout:
        assert b == 1 and bb == 1
        qkv_shape = [
            jax.ShapeDtypeStruct((a, n), F32),
            jax.ShapeDtypeStruct((a, n), F32),
            jax.ShapeDtypeStruct((HEAD_TILES, n, LANES), BF16),
            jax.ShapeDtypeStruct((nt, HEAD_TILES, LANES, tt), BF16),
            jax.ShapeDtypeStruct((HEAD_TILES, n, LANES), BF16),
        ]
        qkv_specs = [
            pl.BlockSpec((a, rows), col_idx),
            pl.BlockSpec((a, rows), col_idx),
            pl.BlockSpec((HEAD_TILES, rows, LANES), tile_idx),
            pl.BlockSpec((1, HEAD_TILES, LANES, tt), lambda bi, ti: (ti, 0, 0, 0)),
            pl.BlockSpec((HEAD_TILES, rows, LANES), tile_idx),
        ]
    else:
        qkv_shape = [jax.ShapeDtypeStruct((n, a), F32)] * 3
        qkv_specs = [pl.BlockSpec((rows, a), row_idx)] * 3
    out_shape = qkv_shape + [
        jax.ShapeDtypeStruct((n, gate_w), F32),
        jax.ShapeDtypeStruct((n, width), BF16),
        jax.ShapeDtypeStruct((b, hist, width), F32),
        jax.ShapeDtypeStruct((b, width), F32),
    ]
    out_specs = qkv_specs + [
        pl.BlockSpec((rows, gate_w), row_idx),
        pl.BlockSpec((rows, width), row_idx),
        pl.BlockSpec((bb, hist, width), lambda bi, ti: (bi, 0, 0)),
        pl.BlockSpec((bb, width), lambda bi, ti: (bi, 0)),
    ]
    return pl.pallas_call(
        functools.partial(_pre_kernel, prompt_layout=prompt_layout),
        grid=(nb, nt),
        in_specs=in_specs,
        out_specs=out_specs,
        out_shape=out_shape,
        scratch_shapes=[pltpu.VMEM((bb, SUBLANES + tt, width), F32), pltpu.VMEM((bb, width), F32)],
        compiler_params=pltpu.CompilerParams(dimension_semantics=("arbitrary", "arbitrary"),
                                             vmem_limit_bytes=VMEM_LIMIT_BYTES),
        name="pre",
    )(x, p["norm_mix_g"], p["w_in"], p["b_gate"], p["lru_conv_w"], p["lru_conv_b"], p["w_lru_a"], p["b_lru_a"],
      p["w_lru_x"], p["b_lru_x"], p["lru_lambda"], conv0, h0)


def _sb_tile(nz, tri, carry, mask):
    lk = _log2_sigmoid(nz)
    if mask is not None:
        lk = jnp.where(mask, lk, 0.0)
    cin = jnp.dot(lk.astype(BF16), tri, preferred_element_type=F32) + carry
    w = jnp.exp2(cin - nz)
    if mask is not None:
        w = jnp.where(mask, w, 0.0)
    return w, cin[:, 0:1]


def _attn_prompt_kernel(bias_ref, q_ref, kt_ref, v_ref, tri_ref, o_ref,
                        qs_ref, nz_buf, lk_buf, cin_buf, acc_ref, carry_ref):
    iq = pl.program_id(0)
    n_tiles, tq, _ = q_ref.shape
    tk = tri_ref.shape[0]
    rows = HEADS_PER_TILE * tq
    tri = tri_ref[...]
    first = lax.broadcasted_iota(jnp.int32, (tq, LANES), 1) < HEAD_DIM

    def neg_logits(g, j):
        nz = jnp.dot(qs_ref[g], kt_ref[j, g], preferred_element_type=F32)
        return jnp.concatenate([nz[e * tq:(e + 1) * tq] - bias_ref[HEADS_PER_TILE * g + e] * LOG2E
                                for e in range(HEADS_PER_TILE)], axis=0)

    def values(g, j):
        return v_ref[g, pl.ds(pl.multiple_of(j * tk, tk), tk), :]

    row = lax.broadcasted_iota(jnp.int32, (HEADS_PER_TILE, tq, tk), 1).reshape(rows, tk)
    col = lax.broadcasted_iota(jnp.int32, (rows, tk), 1)
    causal = col < row
    for g in range(n_tiles):
        qp = q_ref[g]
        zero = jnp.zeros_like(qp)
        qs_ref[g] = jnp.concatenate([jnp.where(first, qp, zero), jnp.where(first, zero, qp)], axis=0)
        w, carry = _sb_tile(neg_logits(g, iq), tri, jnp.zeros((rows, 1), F32), causal)
        acc_ref[g] = jnp.dot(w.astype(BF16), values(g, iq), preferred_element_type=F32)
        carry_ref[g] = carry

    def advance(u):
        g, jj = u
        wrap = jj + 1 >= iq
        return (jnp.where(wrap, g + 1, g), jnp.where(wrap, 0, jj + 1))

    def addr(u):
        g, jj = u
        return jnp.minimum(g, n_tiles - 1), jnp.maximum(iq - 1 - jj, 0)

    def iteration(slot, units, stages):
        ua, uc = units[0], units[2]
        ga, ja = addr(ua)
        gc, jc = addr(uc)
        nz = neg_logits(ga, ja)
        if "tri" in stages:
            cin_buf[1 - slot] = jnp.dot(lk_buf[1 - slot], tri, preferred_element_type=F32)
        if "w" in stages:
            carry = carry_ref[gc]
            cin = cin_buf[slot] + carry
            w = jnp.exp2(cin - nz_buf[slot])
            acc_ref[gc] = acc_ref[gc] + jnp.dot(w.astype(BF16), values(gc, jc), preferred_element_type=F32)
            carry_ref[gc] = cin[:, 0:1]
        nz_buf[slot] = nz
        lk_buf[slot] = _log2_sigmoid(nz).astype(BF16)
        return (advance(ua), ua, units[1])

    def trip(n_iterations):
        def body(_, units):
            for k in range(n_iterations):
                units = iteration(k % 2, units, ("tri", "w"))
            return units
        return body

    @pl.when(iq > 0)
    def _():
        zero = jnp.int32(0)
        units = ((zero, zero),) * 3
        units = iteration(0, units, ())
        units = iteration(1, units, ("tri",))
        n_units = n_tiles * iq
        units = lax.fori_loop(0, n_units // LONG_TRIP, trip(LONG_TRIP), units)
        lax.fori_loop(0, (n_units % LONG_TRIP) // SHORT_TRIP, trip(SHORT_TRIP), units)

    for g in range(n_tiles):
        acc = acc_ref[g]
        o_ref[g] = jnp.where(first, acc[0:tq], acc[tq:rows]).astype(o_ref.dtype)


def _attn_prompt_call(q, kt, v, bias, tri, *, tq):
    n_tiles, t, _ = q.shape
    nt, _, _, tk = kt.shape
    assert tq == tk and n_tiles % SHORT_TRIP == 0 and SHORT_TRIP % 2 == 0
    rows = HEADS_PER_TILE * tq
    return pl.pallas_call(
        _attn_prompt_kernel,
        grid=(t // tq,),
        in_specs=[
            pl.BlockSpec(memory_space=pltpu.SMEM),
            pl.BlockSpec((n_tiles, tq, LANES), lambda i: (0, i, 0)),
            _const_spec(kt.shape),
            _const_spec(v.shape),
            _const_spec(tri.shape),
        ],
        out_specs=pl.BlockSpec((n_tiles, tq, LANES), lambda i: (0, i, 0)),
        out_shape=jax.ShapeDtypeStruct((n_tiles, t, LANES), BF16),
        scratch_shapes=[
            pltpu.VMEM((n_tiles, rows, LANES), BF16),
            pltpu.VMEM((2, rows, tk), F32),
            pltpu.VMEM((2, rows, tk), BF16),
            pltpu.VMEM((2, rows, tk), F32),
            pltpu.VMEM((n_tiles, rows, LANES), F32),
            pltpu.VMEM((n_tiles, rows, 1), F32),
        ],
        compiler_params=pltpu.CompilerParams(dimension_semantics=("arbitrary",),
                                             vmem_limit_bytes=VMEM_LIMIT_BYTES),
        name="attn_prompt",
    )(bias, q, kt, v, tri)


K_PAGES, V_PAGES = 0, 1


def _attn_sample_kernel(pt_ref, bias_ref, q_ref, kn_ref, vn_ref, tri_ref, ck_hbm, cv_hbm, o_ref,
                        kbuf, vbuf, sems, qbd_ref, bcol_ref, kpad_ref, vpad_ref, nz_buf, lk_buf, cin_buf, nzw_buf,
                        acc_ref, carry_ref, *, pages):
    b = pl.program_id(0)
    t, a = q_ref.shape
    rows = N_HEADS * t
    n_pages = pt_ref.shape[1]
    n_groups = n_pages // pages
    tri = tri_ref[...]
    nt_dims = (((1,), (1,)), ((), ()))
    lane = lax.broadcasted_iota(jnp.int32, (t, a), 1)

    def own_lanes(h):
        return (lane >= h * HEAD_DIM) & (lane < (h + 1) * HEAD_DIM)

    def page_copies(kind, seq, grp, slot):
        hbm, buf = (ck_hbm, kbuf) if kind == K_PAGES else (cv_hbm, vbuf)
        return [pltpu.make_async_copy(hbm.at[pt_ref[seq, n_pages - 1 - (grp * pages + j)]], buf.at[slot, j],
                                      sems.at[kind, slot, j]) for j in range(pages)]

    def start(kind, seq, grp, slot):
        for copy in page_copies(kind, seq, grp, slot):
            copy.start()

    def wait(kind, seq, grp, slot):
        for copy in page_copies(kind, seq, grp, slot):
            copy.wait()

    def weights(nz, cin0, carry, mask):
        cin = cin0 + carry
        w = jnp.exp2(cin - nz)
        if mask is not None:
            w = jnp.where(mask, w, 0.0)
        return w, cin[:, 0:1]

    page_cols = [slice(j * PAGE_SIZE, (j + 1) * PAGE_SIZE) for j in range(pages)]

    def stage_z(slot):
        nzs = [jnp.dot(qbd_ref[...], kbuf[slot, j], preferred_element_type=F32) for j in range(pages)]
        for nz, cols in zip(nzs, page_cols):
            nz = nz - bcol_ref[...]
            nz_buf[slot, :, cols] = nz
            lk_buf[slot, :, cols] = _log2_sigmoid(nz).astype(BF16)

    def stage_tri(slot):
        for cols in page_cols:
            cin_buf[slot, :, cols] = jnp.dot(lk_buf[slot, :, cols], tri, preferred_element_type=F32)
        nzw_buf[slot] = nz_buf[slot]

    def stage_w(slot):
        acc = acc_ref[...]
        carry = carry_ref[...]
        for j, cols in enumerate(page_cols):
            w, carry = weights(nzw_buf[slot, :, cols], cin_buf[slot, :, cols], carry, None)
            acc = acc + lax.dot_general(w, vbuf[slot, j], nt_dims, preferred_element_type=F32)
        acc_ref[...] = acc
        carry_ref[...] = carry

    @pl.when(b == 0)
    def _():
        start(K_PAGES, b, 0, 0)
        start(K_PAGES, b, 1, 1)
        start(V_PAGES, b, 0, 0)

    q = q_ref[...]
    for h in range(N_HEADS):
        qbd_ref[h * t:(h + 1) * t, :] = jnp.where(own_lanes(h), q, 0.0)
        bcol_ref[h * t:(h + 1) * t, :] = jnp.full((t, PAGE_SIZE), bias_ref[h] * LOG2E, F32)
    kpad_ref[...] = jnp.zeros_like(kpad_ref)
    vpad_ref[...] = jnp.zeros_like(vpad_ref)
    kpad_ref[0:t, :] = kn_ref[...]
    vpad_ref[0:t, :] = vn_ref[...]
    qpos = lax.broadcasted_iota(jnp.int32, (N_HEADS, t, PAGE_SIZE), 1).reshape(rows, PAGE_SIZE)
    kpos = lax.broadcasted_iota(jnp.int32, (rows, PAGE_SIZE), 1)
    causal = kpos < qpos
    nz = lax.dot_general(qbd_ref[...], kpad_ref[...], nt_dims, preferred_element_type=F32) - bcol_ref[...]
    lk = jnp.where(causal, _log2_sigmoid(nz), 0.0)
    w, carry = weights(nz, jnp.dot(lk.astype(BF16), tri, preferred_element_type=F32),
                       jnp.zeros((rows, 1), F32), causal)
    acc_ref[...] = jnp.dot(w, vpad_ref[...], preferred_element_type=F32)
    carry_ref[...] = carry

    def iteration(s, slot, stages):
        if "next_k" in stages:
            start(K_PAGES, b, s + 1, 1 - slot)
        if "next_v" in stages:
            start(V_PAGES, b, s - 1, 1 - slot)
        if "z" in stages:
            wait(K_PAGES, b, s, slot)
        if "w" in stages:
            wait(V_PAGES, b, s - 2, slot)
        if "z" in stages:
            stage_z(slot)
        if "tri" in stages:
            stage_tri(1 - slot)
        if "w" in stages:
            stage_w(slot)

    def start_for_next_sequence(kind, grp, slot):
        @pl.when(b + 1 < pl.num_programs(0))
        def _():
            start(kind, b + 1, grp, slot)

    iteration(0, 0, ("z",))
    iteration(1, 1, ("next_k", "z", "tri"))

    def body(ii, carry_unused):
        for k in range(2):
            iteration(2 * ii + 2 + k, k, ("next_k", "next_v", "z", "tri", "w"))
        return carry_unused

    lax.fori_loop(0, (n_groups - 4) // 2, body, 0)
    iteration(n_groups - 2, 0, ("next_k", "next_v", "z", "tri", "w"))
    start_for_next_sequence(K_PAGES, 0, 0)
    iteration(n_groups - 1, 1, ("next_v", "z", "tri", "w"))
    start_for_next_sequence(K_PAGES, 1, 1)
    iteration(n_groups, 0, ("next_v", "tri", "w"))
    start_for_next_sequence(V_PAGES, 0, 0)
    iteration(n_groups + 1, 1, ("w",))

    acc = acc_ref[...]
    out = jnp.zeros((t, a), F32)
    for h in range(N_HEADS):
        out = jnp.where(own_lanes(h), acc[h * t:(h + 1) * t, :], out)
    o_ref[...] = out


def _attn_sample_call(q, k, v, cache_kt, cache_vt, page_table, bias, tri, *, t, pages):
    n, a = q.shape
    b = n // t
    n_pages = page_table.shape[1]
    assert n_pages % (2 * pages) == 0 and n_pages // pages >= 4
    rows = N_HEADS * t

    def seq_idx(bi, pt):
        return (bi, 0)

    grid_spec = pltpu.PrefetchScalarGridSpec(
        num_scalar_prefetch=1,
        grid=(b,),
        in_specs=[pl.BlockSpec(memory_space=pltpu.SMEM)] + [pl.BlockSpec((t, a), seq_idx)] * 3
        + [pl.BlockSpec(tri.shape, lambda bi, pt: (0, 0)),
           pl.BlockSpec(memory_space=pl.ANY), pl.BlockSpec(memory_space=pl.ANY)],
        out_specs=pl.BlockSpec((t, a), seq_idx),
        scratch_shapes=[
            pltpu.VMEM((2, pages, a, PAGE_SIZE), F32),
            pltpu.VMEM((2, pages, a, PAGE_SIZE), F32),
            pltpu.SemaphoreType.DMA((2, 2, pages)),
            pltpu.VMEM((rows, a), F32),
            pltpu.VMEM((rows, PAGE_SIZE), F32),
            pltpu.VMEM((PAGE_SIZE, a), F32),
            pltpu.VMEM((PAGE_SIZE, a), F32),
            pltpu.VMEM((2, rows, pages * PAGE_SIZE), F32),
            pltpu.VMEM((2, rows, pages * PAGE_SIZE), BF16),
            pltpu.VMEM((2, rows, pages * PAGE_SIZE), F32),
            pltpu.VMEM((2, rows, pages * PAGE_SIZE), F32),
            pltpu.VMEM((rows, a), F32),
            pltpu.VMEM((rows, 1), F32),
        ],
    )
    return pl.pallas_call(
        functools.partial(_attn_sample_kernel, pages=pages),
        grid_spec=grid_spec,
        out_shape=jax.ShapeDtypeStruct((n, a), F32),
        compiler_params=pltpu.CompilerParams(dimension_semantics=("arbitrary",),
                                             vmem_limit_bytes=VMEM_LIMIT_BYTES),
        name="attn_sample",
    )(page_table, bias, q, k, v, tri, cache_kt, cache_vt)


def _post_kernel(x_ref, oa_ref, ol_ref, gates_ref, wao_ref, wlo_ref, wout_ref, gffn_ref, wg_ref, wu_ref, wd_ref,
                 fcw_ref, fcb_ref, fconv0_ref, gfin_ref,
                 y_ref, fconvn_ref,
                 gpad_ref, *, chunk, final_norm):
    ti = pl.program_id(1)
    bb, tt, d = x_ref.shape
    rows = bb * tt
    ffn = wg_ref.shape[1]
    hist = fcw_ref.shape[0] - 1

    @pl.when(ti == 0)
    def _():
        gpad_ref[:, SUBLANES - hist:SUBLANES, :] = fconv0_ref[...]

    if len(oa_ref.shape) == 3:
        oa = jnp.concatenate([oa_ref[g] for g in range(oa_ref.shape[0])], axis=1)
    else:
        oa = oa_ref[...]
    ao = jnp.dot(oa.astype(BF16), wao_ref[...], preferred_element_type=F32)
    lo = jnp.dot(ol_ref[...], wlo_ref[...], preferred_element_type=F32)
    mixed = gates_ref[:, 0:d] * ao + gates_ref[:, d:2 * d] * lo
    hres = x_ref[...].reshape(rows, d) + jnp.dot(mixed.astype(BF16), wout_ref[...], preferred_element_type=F32)
    hn = _rms(hres, gffn_ref[...]).astype(BF16)

    acc = jnp.zeros((rows, d), F32)
    for c in range(ffn // chunk):
        cs = slice(c * chunk, (c + 1) * chunk)
        gp = jnp.dot(hn, wg_ref[:, cs], preferred_element_type=F32).reshape(bb, tt, chunk)
        gpad_ref[:, SUBLANES:SUBLANES + tt, cs] = gp
        gc = fcb_ref[:, cs].reshape(1, 1, chunk)
        for j in range(hist + 1):
            gc = gc + (gpad_ref[:, SUBLANES - hist + j:SUBLANES - hist + j + tt, cs]
                       * fcw_ref[j:j + 1, cs].reshape(1, 1, chunk))
        tail = gpad_ref[:, SUBLANES - hist + tt:SUBLANES + tt, cs]
        fconvn_ref[:, :, cs] = tail
        gpad_ref[:, SUBLANES - hist:SUBLANES, cs] = tail
        up = jnp.dot(hn, wu_ref[:, cs], preferred_element_type=F32)
        act = jax.nn.gelu(gc.reshape(rows, chunk), approximate=True) * up
        acc = acc + jnp.dot(act.astype(BF16), wd_ref[cs, :], preferred_element_type=F32)

    out = hres + acc
    if final_norm:
        out = _rms(out, gfin_ref[...])
    y_ref[...] = out.reshape(bb, tt, d)


def _post_call(x, oa, ol, gates, fconv0, p, g_final, *, bb, tt, chunk, final_norm):
    b, t, d = x.shape
    nb, nt = b // bb, t // tt
    rows = bb * tt
    ffn = p["w_ffn_gate"].shape[1]
    hist = fconv0.shape[1]
    width = ol.shape[1]

    def row_idx(bi, ti):
        return (bi * nt + ti, 0)

    if oa.ndim == 3:
        oa_spec = pl.BlockSpec((oa.shape[0], rows, oa.shape[2]), lambda bi, ti: (0, bi * nt + ti, 0))
    else:
        oa_spec = pl.BlockSpec((rows, oa.shape[1]), row_idx)
    in_specs = [
        pl.BlockSpec((bb, tt, d), lambda bi, ti: (bi, ti, 0)),
        oa_spec,
        pl.BlockSpec((rows, width), row_idx),
        pl.BlockSpec((rows, 2 * d), row_idx),
        _const_spec((ATTN_WIDTH, d)),
        _const_spec((width, d)),
        _const_spec((d, d)),
        _const_spec((1, d)),
        _const_spec((d, ffn)),
        _const_spec((d, ffn)),
        _const_spec((ffn, d)),
        _const_spec((hist + 1, ffn)),
        _const_spec((1, ffn)),
        pl.BlockSpec((bb, hist, ffn), lambda bi, ti: (bi, 0, 0)),
        _const_spec((1, d)),
    ]
    return pl.pallas_call(
        functools.partial(_post_kernel, chunk=chunk, final_norm=final_norm),
        grid=(nb, nt),
        in_specs=in_specs,
        out_specs=[pl.BlockSpec((bb, tt, d), lambda bi, ti: (bi, ti, 0)),
                   pl.BlockSpec((bb, hist, ffn), lambda bi, ti: (bi, 0, 0))],
        out_shape=[jax.ShapeDtypeStruct((b, t, d), F32), jax.ShapeDtypeStruct((b, hist, ffn), F32)],
        scratch_shapes=[pltpu.VMEM((bb, SUBLANES + tt, ffn), F32)],
        compiler_params=pltpu.CompilerParams(dimension_semantics=("arbitrary", "arbitrary"),
                                             vmem_limit_bytes=VMEM_LIMIT_BYTES),
        name="post",
    )(x, oa, ol, gates, p["w_attn_o"], p["w_lru_o"], p["w_out"], p["norm_ffn_g"], p["w_ffn_gate"], p["w_ffn_up"],
      p["w_ffn_down"], p["ffn_conv_w"], p["ffn_conv_b"], fconv0, g_final)


def _block_diag(w):
    n, d, e = w.shape
    eye = jnp.eye(n, dtype=w.dtype)
    return (w[:, :, None, :] * eye[:, None, :, None]).reshape(n * d, n * e)


def _tri(n):
    j = lax.broadcasted_iota(jnp.int32, (n, n), 0)
    s = lax.broadcasted_iota(jnp.int32, (n, n), 1)
    return (j >= s).astype(BF16)


def _layer_params(l, norm_mix_g, w_in, b_gate, w_attn_o, w_lru_o, w_out, lru_conv_w, lru_conv_b,
                  w_lru_a, b_lru_a, w_lru_x, b_lru_x, lru_lambda, norm_ffn_g, w_ffn_gate, w_ffn_up,
                  ffn_conv_w, ffn_conv_b, w_ffn_down):
    row = lambda v: v[l].reshape(1, -1)
    return dict(
        norm_mix_g=row(norm_mix_g), w_in=w_in[l].astype(BF16), b_gate=row(b_gate),
        w_attn_o=w_attn_o[l].astype(BF16), w_lru_o=w_lru_o[l].astype(BF16),
        w_out=w_out[l].astype(BF16), lru_conv_w=lru_conv_w[l], lru_conv_b=row(lru_conv_b),
        w_lru_a=_block_diag(w_lru_a[l]).astype(BF16), b_lru_a=row(b_lru_a),
        w_lru_x=_block_diag(w_lru_x[l]).astype(BF16), b_lru_x=row(b_lru_x), lru_lambda=row(lru_lambda),
        norm_ffn_g=row(norm_ffn_g), w_ffn_gate=w_ffn_gate[l].astype(BF16), w_ffn_up=w_ffn_up[l].astype(BF16),
        ffn_conv_w=ffn_conv_w[l], ffn_conv_b=row(ffn_conv_b), w_ffn_down=w_ffn_down[l].astype(BF16))


def kernel(x_prompt, x_sample, cache_k, cache_v, page_table, state_lru_conv, state_lru_h, state_ffn_conv,
           norm_mix_g, w_in, b_gate, attn_logit_bias, w_attn_o, w_lru_o, w_out, lru_conv_w, lru_conv_b,
           w_lru_a, b_lru_a, w_lru_x, b_lru_x, lru_lambda, norm_ffn_g, w_ffn_gate, w_ffn_up,
           ffn_conv_w, ffn_conv_b, w_ffn_down, norm_final_g):
    depth = w_in.shape[0]
    bp, tp, d = x_prompt.shape
    bs, ts, _ = x_sample.shape
    width = state_lru_h.shape[-1]
    ffn = state_ffn_conv.shape[-1]
    n_pool = cache_k.shape[1]
    n_pages = page_table.shape[1]
    g_final = norm_final_g.reshape(1, d)

    tile_p = min(256, tp)
    tile_post = min(512, tp)
    seq_blk = min(32, bs)
    chunk = 512
    pages = next(c for c in (8, 4, 2, 1) if n_pages % (2 * c) == 0 and n_pages // c >= 4)
    tri_p = _tri(tile_p)
    tri_s = _tri(PAGE_SIZE)

    hp, hs = x_prompt, x_sample
    outs = [[] for _ in range(10)]
    for l in range(depth):
        p = _layer_params(l, norm_mix_g, w_in, b_gate, w_attn_o, w_lru_o, w_out, lru_conv_w, lru_conv_b,
                          w_lru_a, b_lru_a, w_lru_x, b_lru_x, lru_lambda, norm_ffn_g, w_ffn_gate, w_ffn_up,
                          ffn_conv_w, ffn_conv_b, w_ffn_down)
        bias = attn_logit_bias[l]
        last = l == depth - 1

        ktp, vtp, q, kt, v, gates, olru, lcp, lhp = _pre_call(
            hp, jnp.zeros((bp, lru_conv_w.shape[1] - 1, width), F32), jnp.zeros((bp, width), F32), p,
            bb=1, tt=tile_p, prompt_layout=True)
        oa = _attn_prompt_call(q, kt, v, bias, tri_p, tq=tile_p)
        hp, fcp = _post_call(hp, oa, olru, gates, jnp.zeros((bp, ffn_conv_w.shape[1] - 1, ffn), F32), p, g_final,
                             bb=1, tt=tile_post, chunk=chunk, final_norm=last)

        ks, vs, q, gates, olru, lcs, lhs = _pre_call(
            hs, state_lru_conv[l], state_lru_h[l], p, bb=seq_blk, tt=ts, prompt_layout=False)
        ckt = jnp.transpose(cache_k[l], (0, 2, 3, 1)).reshape(n_pool, ATTN_WIDTH, PAGE_SIZE)
        cvt = jnp.transpose(cache_v[l], (0, 2, 3, 1)).reshape(n_pool, ATTN_WIDTH, PAGE_SIZE)
        oa = _attn_sample_call(q, ks, vs, ckt, cvt, page_table, bias, tri_s, t=ts, pages=pages)
        hs, fcs = _post_call(hs, oa, olru, gates, state_ffn_conv[l], p, g_final,
                             bb=seq_blk, tt=ts, chunk=chunk, final_norm=last)

        from_t = lambda m: jnp.transpose(m.reshape(N_HEADS, HEAD_DIM, bp, tp), (2, 3, 0, 1))
        layer = (from_t(ktp), ks.reshape(bs, ts, N_HEADS, HEAD_DIM),
                 from_t(vtp), vs.reshape(bs, ts, N_HEADS, HEAD_DIM),
                 lcp, lcs, lhp, lhs, fcp, fcs)
        for acc, val in zip(outs, layer):
            acc.append(val)
    return (hp, hs) + tuple(jnp.stack(o) for o in outs)
```

```python
import functools

import jax
import jax.numpy as jnp
from jax import lax
from jax.experimental import pallas as pl
from jax.experimental.pallas import tpu as pltpu

N_HEADS = 8
HEAD_DIM = 64
ATTN_WIDTH = N_HEADS * HEAD_DIM
LRU_C = 8.0
RMS_EPS = 1e-6
PAGE_SIZE = 128

LOG2E = 1.4426950408889634

F32 = jnp.float32
BF16 = jnp.bfloat16

VMEM_LIMIT_BYTES = 56 * 1024 * 1024
SUBLANES = 8
LANES = 128
HEADS_PER_TILE = LANES // HEAD_DIM
HEAD_TILES = ATTN_WIDTH // LANES
SHORT_TRIP = HEAD_TILES
LONG_TRIP = 2 * SHORT_TRIP


def _const_spec(shape):
    zeros = (0,) * len(shape)
    return pl.BlockSpec(shape, lambda *_: zeros, pipeline_mode=pl.Buffered(1))


def _log_sigmoid(x):
    return jnp.minimum(x, 0.0) - jnp.log(1.0 + jnp.exp2(jnp.abs(x) * (-LOG2E)))


def _log2_sigmoid(x2):
    neg_abs = pltpu.bitcast(pltpu.bitcast(x2, jnp.uint32) | jnp.uint32(0x80000000), F32)
    return jnp.minimum(x2, 0.0) - jnp.log2(1.0 + jnp.exp2(neg_abs))


def _rms(x, g):
    ms = jnp.mean(x * x, axis=-1, keepdims=True)
    return (x * lax.rsqrt(ms + RMS_EPS)) * g


def _pre_kernel(x_ref, g_ref, win_ref, bgate_ref, cw_ref, cb_ref, wa_ref, ba_ref, wx_ref, bx_ref, lam_ref,
                conv0_ref, h0_ref, *refs, prompt_layout):
    qkv_refs = refs[:-6]
    gates_ref, olru_ref, convn_ref, hlast_ref, xpad_ref, hcarry_ref = refs[-6:]
    ti = pl.program_id(1)
    bb, tt, d = x_ref.shape
    rows = bb * tt
    width = xpad_ref.shape[-1]
    hist = cw_ref.shape[0] - 1

    @pl.when(ti == 0)
    def _():
        xpad_ref[:, SUBLANES - hist:SUBLANES, :] = conv0_ref[...]
        hcarry_ref[...] = h0_ref[...]

    x = x_ref[...].reshape(rows, d)
    xn = _rms(x, g_ref[...])
    proj = jnp.dot(xn.astype(BF16), win_ref[...], preferred_element_type=F32)

    a = ATTN_WIDTH
    qf = proj[:, 0:a] * -(HEAD_DIM ** -0.5 * LOG2E)
    kf = proj[:, a:2 * a]
    vf = proj[:, 2 * a:3 * a]
    if prompt_layout:
        ktf_ref, vtf_ref, q_ref, kt_ref, v_ref = qkv_refs
        kt = kf.T
        ktf_ref[...] = kt
        vtf_ref[...] = vf.T
        kt_ref[0] = kt.astype(BF16).reshape(HEAD_TILES, LANES, rows)
        for g in range(HEAD_TILES):
            q_ref[g] = qf[:, g * LANES:(g + 1) * LANES].astype(BF16)
            v_ref[g] = vf[:, g * LANES:(g + 1) * LANES].astype(BF16)
    else:
        k_ref, v_ref, q_ref = qkv_refs
        k_ref[...] = kf
        v_ref[...] = vf
        q_ref[...] = qf

    gates_ref[...] = jax.nn.sigmoid(proj[:, 3 * a + width:] + bgate_ref[...])

    xl = proj[:, 3 * a:3 * a + width].reshape(bb, tt, width)
    xpad_ref[:, SUBLANES:SUBLANES + tt, :] = xl
    xc = cb_ref[...].reshape(1, 1, width)
    for j in range(hist + 1):
        xc = xc + xpad_ref[:, SUBLANES - hist + j:SUBLANES - hist + j + tt, :] * cw_ref[j:j + 1, :].reshape(1, 1, width)
    tail = xpad_ref[:, SUBLANES - hist + tt:SUBLANES + tt, :]
    convn_ref[...] = tail
    xpad_ref[:, SUBLANES - hist:SUBLANES, :] = tail

    xc2 = xc.reshape(rows, width)
    xcb = xc2.astype(BF16)
    r = jax.nn.sigmoid(jnp.dot(xcb, wa_ref[...], preferred_element_type=F32) + ba_ref[...])
    i = jax.nn.sigmoid(jnp.dot(xcb, wx_ref[...], preferred_element_type=F32) + bx_ref[...])
    log_a = LRU_C * r * _log_sigmoid(lam_ref[...])
    av = jnp.exp(log_a)
    bv = jnp.sqrt(1.0 - av * av) * (i * xc2)

    A = av.reshape(bb, tt, width)
    B = bv.reshape(bb, tt, width)
    tidx = lax.broadcasted_iota(jnp.int32, (bb, tt, width), 1)
    s = 1
    while s < tt:
        a_sh = pltpu.roll(A, s, axis=1)
        b_sh = pltpu.roll(B, s, axis=1)
        valid = tidx >= s
        B = jnp.where(valid, A * b_sh + B, B)
        A = jnp.where(valid, A * a_sh, A)
        s *= 2
    hs = A * hcarry_ref[...].reshape(bb, 1, width) + B
    olru_ref[...] = hs.reshape(rows, width).astype(olru_ref.dtype)
    h_last = hs[:, tt - 1, :]
    hcarry_ref[...] = h_last
    hlast_ref[...] = h_last


def _pre_call(x, conv0, h0, p, *, bb, tt, prompt_layout):
    b, t, d = x.shape
    nb, nt = b // bb, t // tt
    rows = bb * tt
    n = b * t
    width = conv0.shape[-1]
    hist = conv0.shape[1]
    in_width = p["w_in"].shape[1]
    gate_w = in_width - 3 * ATTN_WIDTH - width
    a = ATTN_WIDTH

    def row_idx(bi, ti):
        return (bi * nt + ti, 0)

    def col_idx(bi, ti):
        return (0, bi * nt + ti)

    def tile_idx(bi, ti):
        return (0, bi * nt + ti, 0)

    in_specs = [
        pl.BlockSpec((bb, tt, d), lambda bi, ti: (bi, ti, 0)),
        _const_spec((1, d)),
        _const_spec((d, in_width)),
        _const_spec((1, gate_w)),
        _const_spec((hist + 1, width)),
        _const_spec((1, width)),
        _const_spec((width, width)),
        _const_spec((1, width)),
        _const_spec((width, width)),
        _const_spec((1, width)),
        _const_spec((1, width)),
        pl.BlockSpec((bb, hist, width), lambda bi, ti: (bi, 0, 0)),
        pl.BlockSpec((bb, width), lambda bi, ti: (bi, 0)),
    ]
    if prompt_layout:
        assert b == 1 and bb == 1
        qkv_shape = [
            jax.ShapeDtypeStruct((a, n), F32),
            jax.ShapeDtypeStruct((a, n), F32),
            jax.ShapeDtypeStruct((HEAD_TILES, n, LANES), BF16),
            jax.ShapeDtypeStruct((nt, HEAD_TILES, LANES, tt), BF16),
            jax.ShapeDtypeStruct((HEAD_TILES, n, LANES), BF16),
        ]
        qkv_specs = [
            pl.BlockSpec((a, rows), col_idx),
            pl.BlockSpec((a, rows), col_idx),
            pl.BlockSpec((HEAD_TILES, rows, LANES), tile_idx),
            pl.BlockSpec((1, HEAD_TILES, LANES, tt), lambda bi, ti: (ti, 0, 0, 0)),
            pl.BlockSpec((HEAD_TILES, rows, LANES), tile_idx),
        ]
    else:
        qkv_shape = [jax.ShapeDtypeStruct((n, a), F32)] * 3
        qkv_specs = [pl.BlockSpec((rows, a), row_idx)] * 3
    out_shape = qkv_shape + [
        jax.ShapeDtypeStruct((n, gate_w), F32),
        jax.ShapeDtypeStruct((n, width), BF16),
        jax.ShapeDtypeStruct((b, hist, width), F32),
        jax.ShapeDtypeStruct((b, width), F32),
    ]
    out_specs = qkv_specs + [
        pl.BlockSpec((rows, gate_w), row_idx),
        pl.BlockSpec((rows, width), row_idx),
        pl.BlockSpec((bb, hist, width), lambda bi, ti: (bi, 0, 0)),
        pl.BlockSpec((bb, width), lambda bi, ti: (bi, 0)),
    ]
    return pl.pallas_call(
        functools.partial(_pre_kernel, prompt_layout=prompt_layout),
        grid=(nb, nt),
        in_specs=in_specs,
        out_specs=out_specs,
        out_shape=out_shape,
        scratch_shapes=[pltpu.VMEM((bb, SUBLANES + tt, width), F32), pltpu.VMEM((bb, width), F32)],
        compiler_params=pltpu.CompilerParams(dimension_semantics=("arbitrary", "arbitrary"),
                                             vmem_limit_bytes=VMEM_LIMIT_BYTES),
        name="pre",
    )(x, p["norm_mix_g"], p["w_in"], p["b_gate"], p["lru_conv_w"], p["lru_conv_b"], p["w_lru_a"], p["b_lru_a"],
      p["w_lru_x"], p["b_lru_x"], p["lru_lambda"], conv0, h0)


def _sb_tile(nz, tri, carry, mask):
    lk = _log2_sigmoid(nz)
    if mask is not None:
        lk = jnp.where(mask, lk, 0.0)
    cin = jnp.dot(lk.astype(BF16), tri, preferred_element_type=F32) + carry
    w = jnp.exp2(cin - nz)
    if mask is not None:
        w = jnp.where(mask, w, 0.0)
    return w, cin[:, 0:1]


def _attn_prompt_tile(bias_ref, q_ref, kt_ref, v_ref, tri_ref, o_ref,
                      qs_ref, nz_buf, lk_buf, cin_buf, acc_ref, carry_ref, *, tick):
    iq = pl.program_id(0)
    n_tiles, tq, _ = q_ref.shape
    tk = tri_ref.shape[0]
    rows = HEADS_PER_TILE * tq
    tri = tri_ref[...]
    first = lax.broadcasted_iota(jnp.int32, (tq, LANES), 1) < HEAD_DIM

    def neg_logits(g, j):
        nz = jnp.dot(qs_ref[g], kt_ref[j, g], preferred_element_type=F32)
        return jnp.concatenate([nz[e * tq:(e + 1) * tq] - bias_ref[HEADS_PER_TILE * g + e] * LOG2E
                                for e in range(HEADS_PER_TILE)], axis=0)

    def values(g, j):
        return v_ref[g, pl.ds(pl.multiple_of(j * tk, tk), tk), :]

    row = lax.broadcasted_iota(jnp.int32, (HEADS_PER_TILE, tq, tk), 1).reshape(rows, tk)
    col = lax.broadcasted_iota(jnp.int32, (rows, tk), 1)
    causal = col < row
    for g in range(n_tiles):
        qp = q_ref[g]
        zero = jnp.zeros_like(qp)
        qs_ref[g] = jnp.concatenate([jnp.where(first, qp, zero), jnp.where(first, zero, qp)], axis=0)
        w, carry = _sb_tile(neg_logits(g, iq), tri, jnp.zeros((rows, 1), F32), causal)
        acc_ref[g] = jnp.dot(w.astype(BF16), values(g, iq), preferred_element_type=F32)
        carry_ref[g] = carry

    def advance(u):
        g, jj = u
        wrap = jj + 1 >= iq
        return (jnp.where(wrap, g + 1, g), jnp.where(wrap, 0, jj + 1))

    def addr(u):
        g, jj = u
        return jnp.minimum(g, n_tiles - 1), jnp.maximum(iq - 1 - jj, 0)

    def iteration(slot, units, stages):
        ua, uc = units[0], units[2]
        ga, ja = addr(ua)
        gc, jc = addr(uc)
        nz = neg_logits(ga, ja)
        if "tri" in stages:
            cin_buf[1 - slot] = jnp.dot(lk_buf[1 - slot], tri, preferred_element_type=F32)
        if "w" in stages:
            carry = carry_ref[gc]
            cin = cin_buf[slot] + carry
            w = jnp.exp2(cin - nz_buf[slot])
            acc_ref[gc] = acc_ref[gc] + jnp.dot(w.astype(BF16), values(gc, jc), preferred_element_type=F32)
            carry_ref[gc] = cin[:, 0:1]
        nz_buf[slot] = nz
        lk_buf[slot] = _log2_sigmoid(nz).astype(BF16)
        return (advance(ua), ua, units[1])

    def trip(n_iterations):
        def body(_, units):
            for k in range(n_iterations):
                units = iteration(k % 2, units, ("tri", "w"))
                if (k + 1) % SHORT_TRIP == 0:
                    tick()
            return units
        return body

    @pl.when(iq > 0)
    def _():
        zero = jnp.int32(0)
        units = ((zero, zero),) * 3
        units = iteration(0, units, ())
        units = iteration(1, units, ("tri",))
        n_units = n_tiles * iq
        units = lax.fori_loop(0, n_units // LONG_TRIP, trip(LONG_TRIP), units)
        lax.fori_loop(0, (n_units % LONG_TRIP) // SHORT_TRIP, trip(SHORT_TRIP), units)

    for g in range(n_tiles):
        acc = acc_ref[g]
        o_ref[g] = jnp.where(first, acc[0:tq], acc[tq:rows]).astype(o_ref.dtype)


K_PAGES, V_PAGES = 0, 1


def _sample_attention(pt_ref, bias_ref, qkv_hbm, tri_ref, ck_hbm, cv_hbm, o_ref,
                      kbuf, vbuf, sems, qkv_buf, qkv_sems, cnt_ref, qbd_ref, bcol_ref, kpad_ref, vpad_ref,
                      nz_buf, lk_buf, cin_buf, nzw_buf, acc_ref, carry_ref, *, pages):
    _, t, a = qkv_buf.shape
    n_seq = o_ref.shape[0] // t
    rows = N_HEADS * t
    n_pages = pt_ref.shape[1]
    n_groups = n_pages // pages
    tri = tri_ref[...]
    nt_dims = (((1,), (1,)), ((), ()))
    lane = lax.broadcasted_iota(jnp.int32, (t, a), 1)

    def qkv_copies(seq):
        rows_of_seq = pl.ds(pl.multiple_of(seq * t, t), t)
        return [pltpu.make_async_copy(hbm.at[rows_of_seq], qkv_buf.at[i], qkv_sems.at[i])
                for i, hbm in enumerate(qkv_hbm)]

    def own_lanes(h):
        return (lane >= h * HEAD_DIM) & (lane < (h + 1) * HEAD_DIM)

    def page_copies(kind, seq, grp, slot):
        hbm, buf = (ck_hbm, kbuf) if kind == K_PAGES else (cv_hbm, vbuf)
        return [pltpu.make_async_copy(hbm.at[pt_ref[seq, n_pages - 1 - (grp * pages + j)]], buf.at[slot, j],
                                      sems.at[kind, slot, j]) for j in range(pages)]

    def start(kind, seq, grp, slot):
        for copy in page_copies(kind, seq, grp, slot):
            copy.start()

    def wait(kind, seq, grp, slot):
        for copy in page_copies(kind, seq, grp, slot):
            copy.wait()

    def weights(nz, cin0, carry, mask):
        cin = cin0 + carry
        w = jnp.exp2(cin - nz)
        if mask is not None:
            w = jnp.where(mask, w, 0.0)
        return w, cin[:, 0:1]

    page_cols = [slice(j * PAGE_SIZE, (j + 1) * PAGE_SIZE) for j in range(pages)]

    def stage_z(slot):
        nzs = [jnp.dot(qbd_ref[...], kbuf[slot, j], preferred_element_type=F32) for j in range(pages)]
        for nz, cols in zip(nzs, page_cols):
            nz = nz - bcol_ref[...]
            nz_buf[slot, :, cols] = nz
            lk_buf[slot, :, cols] = _log2_sigmoid(nz).astype(BF16)

    def stage_tri(slot):
        for cols in page_cols:
            cin_buf[slot, :, cols] = jnp.dot(lk_buf[slot, :, cols], tri, preferred_element_type=F32)
        nzw_buf[slot] = nz_buf[slot]

    def stage_w(slot):
        acc = acc_ref[...]
        carry = carry_ref[...]
        for j, cols in enumerate(page_cols):
            w, carry = weights(nzw_buf[slot, :, cols], cin_buf[slot, :, cols], carry, None)
            acc = acc + lax.dot_general(w, vbuf[slot, j], nt_dims, preferred_element_type=F32)
        acc_ref[...] = acc
        carry_ref[...] = carry

    def setup(seq):
        for copy in qkv_copies(seq):
            copy.wait()
        q = qkv_buf[0]
        for h in range(N_HEADS):
            qbd_ref[h * t:(h + 1) * t, :] = jnp.where(own_lanes(h), q, 0.0)
            bcol_ref[h * t:(h + 1) * t, :] = jnp.full((t, PAGE_SIZE), bias_ref[h] * LOG2E, F32)
        kpad_ref[...] = jnp.zeros_like(kpad_ref)
        vpad_ref[...] = jnp.zeros_like(vpad_ref)
        kpad_ref[0:t, :] = qkv_buf[1]
        vpad_ref[0:t, :] = qkv_buf[2]
        start_for_next_sequence(seq, lambda nxt: [copy.start() for copy in qkv_copies(nxt)])
        qpos = lax.broadcasted_iota(jnp.int32, (N_HEADS, t, PAGE_SIZE), 1).reshape(rows, PAGE_SIZE)
        kpos = lax.broadcasted_iota(jnp.int32, (rows, PAGE_SIZE), 1)
        causal = kpos < qpos
        nz = lax.dot_general(qbd_ref[...], kpad_ref[...], nt_dims, preferred_element_type=F32) - bcol_ref[...]
        lk = jnp.where(causal, _log2_sigmoid(nz), 0.0)
        w, carry = weights(nz, jnp.dot(lk.astype(BF16), tri, preferred_element_type=F32),
                           jnp.zeros((rows, 1), F32), causal)
        acc_ref[...] = jnp.dot(w, vpad_ref[...], preferred_element_type=F32)
        carry_ref[...] = carry

    def finish(seq):
        acc = acc_ref[...]
        out = jnp.zeros((t, a), F32)
        for h in range(N_HEADS):
            out = jnp.where(own_lanes(h), acc[h * t:(h + 1) * t, :], out)
        o_ref[pl.ds(pl.multiple_of(seq * t, t), t), :] = out

    def iteration(seq, s, slot, stages):
        if "next_k" in stages:
            start(K_PAGES, seq, s + 1, 1 - slot)
        if "next_v" in stages:
            start(V_PAGES, seq, s - 1, 1 - slot)
        if "z" in stages:
            wait(K_PAGES, seq, s, slot)
        if "w" in stages:
            wait(V_PAGES, seq, s - 2, slot)
        if "z" in stages:
            stage_z(slot)
        if "tri" in stages:
            stage_tri(1 - slot)
        if "w" in stages:
            stage_w(slot)

    def start_for_next_sequence(seq, start_fn):
        @pl.when(seq + 1 < n_seq)
        def _():
            start_fn(seq + 1)

    full = ("next_k", "next_v", "z", "tri", "w")
    last = n_groups + 1

    def run(seq, s):
        def case(cond, fn):
            pl.when(cond)(fn)

        def first():
            setup(seq)
            iteration(seq, 0, 0, ("z",))

        def drain_a():
            start_for_next_sequence(seq, lambda nxt: start(K_PAGES, nxt, 0, 0))
            iteration(seq, n_groups - 1, 1, ("next_v", "z", "tri", "w"))

        def drain_b():
            start_for_next_sequence(seq, lambda nxt: start(K_PAGES, nxt, 1, 1))
            iteration(seq, n_groups, 0, ("next_v", "tri", "w"))

        def drain_c():
            start_for_next_sequence(seq, lambda nxt: start(V_PAGES, nxt, 0, 0))
            iteration(seq, last, 1, ("w",))
            finish(seq)

        steady = (s >= 2) & (s <= n_groups - 2)
        case(s == 0, first)
        case(s == 1, lambda: iteration(seq, 1, 1, ("next_k", "z", "tri")))
        case(steady & (s % 2 == 0), lambda: iteration(seq, s, 0, full))
        case(steady & (s % 2 == 1), lambda: iteration(seq, s, 1, full))
        case(s == n_groups - 1, drain_a)
        case(s == n_groups, drain_b)
        case(s == last, drain_c)

    def init():
        cnt_ref[0] = 0
        cnt_ref[1] = 0
        zero = jnp.int32(0)
        start(K_PAGES, zero, 0, 0)
        start(K_PAGES, zero, 1, 1)
        start(V_PAGES, zero, 0, 0)
        for copy in qkv_copies(zero):
            copy.start()

    def tick():
        seq, s = cnt_ref[0], cnt_ref[1]

        @pl.when(seq < n_seq)
        def _():
            run(seq, s)
            done = s == last
            cnt_ref[0] = jnp.where(done, seq + 1, seq)
            cnt_ref[1] = jnp.where(done, 0, s + 1)

    def drain():
        remaining = (n_seq - cnt_ref[0]) * (last + 1) - cnt_ref[1]
        lax.fori_loop(0, remaining, lambda _, c: (tick(), c)[1], 0)

    return init, tick, drain


N_PROMPT_SCRATCH = 6


def _attn_kernel(pt_ref, bias_ref, q_ref, kt_ref, v_ref, tri_ref, qs_hbm, ks_hbm, vs_hbm, tris_ref, ck_hbm, cv_hbm,
                 o_ref, os_ref, *scratch, pages):
    init, tick, drain = _sample_attention(pt_ref, bias_ref, (qs_hbm, ks_hbm, vs_hbm), tris_ref, ck_hbm, cv_hbm,
                                          os_ref, *scratch[N_PROMPT_SCRATCH:], pages=pages)
    iq = pl.program_id(0)
    pl.when(iq == 0)(init)
    _attn_prompt_tile(bias_ref, q_ref, kt_ref, v_ref, tri_ref, o_ref, *scratch[:N_PROMPT_SCRATCH], tick=tick)
    pl.when(iq == pl.num_programs(0) - 1)(drain)


def _attn_call(q, kt, v, tri, qs, ks, vs, tri_s, cache_kt, cache_vt, page_table, bias, *, tq, t, pages):
    n_tiles, tp, _ = q.shape
    _, _, _, tk = kt.shape
    assert tq == tk and n_tiles % SHORT_TRIP == 0 and SHORT_TRIP % 2 == 0
    rows_p = HEADS_PER_TILE * tq
    n, a = qs.shape
    n_pages = page_table.shape[1]
    assert n_pages % (2 * pages) == 0 and n_pages // pages >= 4
    rows_s = N_HEADS * t
    group = pages * PAGE_SIZE

    def const(shape):
        zeros = (0,) * len(shape)
        return pl.BlockSpec(shape, lambda i, pt: zeros, pipeline_mode=pl.Buffered(1))

    hbm = pl.BlockSpec(memory_space=pl.ANY)
    grid_spec = pltpu.PrefetchScalarGridSpec(
        num_scalar_prefetch=1,
        grid=(tp // tq,),
        in_specs=[pl.BlockSpec(memory_space=pltpu.SMEM),
                  pl.BlockSpec((n_tiles, tq, LANES), lambda i, pt: (0, i, 0)),
                  const(kt.shape), const(v.shape), const(tri.shape),
                  hbm, hbm, hbm, const(tri_s.shape), hbm, hbm],
        out_specs=[pl.BlockSpec((n_tiles, tq, LANES), lambda i, pt: (0, i, 0)),
                   pl.BlockSpec((n, a), lambda i, pt: (0, 0))],
        scratch_shapes=[
            pltpu.VMEM((n_tiles, rows_p, LANES), BF16),
            pltpu.VMEM((2, rows_p, tk), F32),
            pltpu.VMEM((2, rows_p, tk), BF16),
            pltpu.VMEM((2, rows_p, tk), F32),
            pltpu.VMEM((n_tiles, rows_p, LANES), F32),
            pltpu.VMEM((n_tiles, rows_p, 1), F32),
            pltpu.VMEM((2, pages, a, PAGE_SIZE), F32),
            pltpu.VMEM((2, pages, a, PAGE_SIZE), F32),
            pltpu.SemaphoreType.DMA((2, 2, pages)),
            pltpu.VMEM((3, t, a), F32),
            pltpu.SemaphoreType.DMA((3,)),
            pltpu.SMEM((2,), jnp.int32),
            pltpu.VMEM((rows_s, a), F32),
            pltpu.VMEM((rows_s, PAGE_SIZE), F32),
            pltpu.VMEM((PAGE_SIZE, a), F32),
            pltpu.VMEM((PAGE_SIZE, a), F32),
            pltpu.VMEM((2, rows_s, group), F32),
            pltpu.VMEM((2, rows_s, group), BF16),
            pltpu.VMEM((2, rows_s, group), F32),
            pltpu.VMEM((2, rows_s, group), F32),
            pltpu.VMEM((rows_s, a), F32),
            pltpu.VMEM((rows_s, 1), F32),
        ],
    )
    return pl.pallas_call(
        functools.partial(_attn_kernel, pages=pages),
        grid_spec=grid_spec,
        out_shape=[jax.ShapeDtypeStruct((n_tiles, tp, LANES), BF16), jax.ShapeDtypeStruct((n, a), F32)],
        compiler_params=pltpu.CompilerParams(dimension_semantics=("arbitrary",),
                                             vmem_limit_bytes=VMEM_LIMIT_BYTES),
        name="attn",
    )(page_table, bias, q, kt, v, tri, qs, ks, vs, tri_s, cache_kt, cache_vt)


def _post_kernel(x_ref, oa_ref, ol_ref, gates_ref, wao_ref, wlo_ref, wout_ref, gffn_ref, wg_ref, wu_ref, wd_ref,
                 fcw_ref, fcb_ref, fconv0_ref, gfin_ref,
                 y_ref, fconvn_ref,
                 gpad_ref, *, chunk, final_norm):
    ti = pl.program_id(1)
    bb, tt, d = x_ref.shape
    rows = bb * tt
    ffn = wg_ref.shape[1]
    hist = fcw_ref.shape[0] - 1

    @pl.when(ti == 0)
    def _():
        gpad_ref[:, SUBLANES - hist:SUBLANES, :] = fconv0_ref[...]

    if len(oa_ref.shape) == 3:
        oa = jnp.concatenate([oa_ref[g] for g in range(oa_ref.shape[0])], axis=1)
    else:
        oa = oa_ref[...]
    ao = jnp.dot(oa.astype(BF16), wao_ref[...], preferred_element_type=F32)
    lo = jnp.dot(ol_ref[...], wlo_ref[...], preferred_element_type=F32)
    mixed = gates_ref[:, 0:d] * ao + gates_ref[:, d:2 * d] * lo
    hres = x_ref[...].reshape(rows, d) + jnp.dot(mixed.astype(BF16), wout_ref[...], preferred_element_type=F32)
    hn = _rms(hres, gffn_ref[...]).astype(BF16)

    acc = jnp.zeros((rows, d), F32)
    for c in range(ffn // chunk):
        cs = slice(c * chunk, (c + 1) * chunk)
        gp = jnp.dot(hn, wg_ref[:, cs], preferred_element_type=F32).reshape(bb, tt, chunk)
        gpad_ref[:, SUBLANES:SUBLANES + tt, cs] = gp
        gc = fcb_ref[:, cs].reshape(1, 1, chunk)
        for j in range(hist + 1):
            gc = gc + (gpad_ref[:, SUBLANES - hist + j:SUBLANES - hist + j + tt, cs]
                       * fcw_ref[j:j + 1, cs].reshape(1, 1, chunk))
        tail = gpad_ref[:, SUBLANES - hist + tt:SUBLANES + tt, cs]
        fconvn_ref[:, :, cs] = tail
        gpad_ref[:, SUBLANES - hist:SUBLANES, cs] = tail
        up = jnp.dot(hn, wu_ref[:, cs], preferred_element_type=F32)
        act = jax.nn.gelu(gc.reshape(rows, chunk), approximate=True) * up
        acc = acc + jnp.dot(act.astype(BF16), wd_ref[cs, :], preferred_element_type=F32)

    out = hres + acc
    if final_norm:
        out = _rms(out, gfin_ref[...])
    y_ref[...] = out.reshape(bb, tt, d)


def _post_call(x, oa, ol, gates, fconv0, p, g_final, *, bb, tt, chunk, final_norm):
    b, t, d = x.shape
    nb, nt = b // bb, t // tt
    rows = bb * tt
    ffn = p["w_ffn_gate"].shape[1]
    hist = fconv0.shape[1]
    width = ol.shape[1]

    def row_idx(bi, ti):
        return (bi * nt + ti, 0)

    if oa.ndim == 3:
        oa_spec = pl.BlockSpec((oa.shape[0], rows, oa.shape[2]), lambda bi, ti: (0, bi * nt + ti, 0))
    else:
        oa_spec = pl.BlockSpec((rows, oa.shape[1]), row_idx)
    in_specs = [
        pl.BlockSpec((bb, tt, d), lambda bi, ti: (bi, ti, 0)),
        oa_spec,
        pl.BlockSpec((rows, width), row_idx),
        pl.BlockSpec((rows, 2 * d), row_idx),
        _const_spec((ATTN_WIDTH, d)),
        _const_spec((width, d)),
        _const_spec((d, d)),
        _const_spec((1, d)),
        _const_spec((d, ffn)),
        _const_spec((d, ffn)),
        _const_spec((ffn, d)),
        _const_spec((hist + 1, ffn)),
        _const_spec((1, ffn)),
        pl.BlockSpec((bb, hist, ffn), lambda bi, ti: (bi, 0, 0)),
        _const_spec((1, d)),
    ]
    return pl.pallas_call(
        functools.partial(_post_kernel, chunk=chunk, final_norm=final_norm),
        grid=(nb, nt),
        in_specs=in_specs,
        out_specs=[pl.BlockSpec((bb, tt, d), lambda bi, ti: (bi, ti, 0)),
                   pl.BlockSpec((bb, hist, ffn), lambda bi, ti: (bi, 0, 0))],
        out_shape=[jax.ShapeDtypeStruct((b, t, d), F32), jax.ShapeDtypeStruct((b, hist, ffn), F32)],
        scratch_shapes=[pltpu.VMEM((bb, SUBLANES + tt, ffn), F32)],
        compiler_params=pltpu.CompilerParams(dimension_semantics=("arbitrary", "arbitrary"),
                                             vmem_limit_bytes=VMEM_LIMIT_BYTES),
        name="post",
    )(x, oa, ol, gates, p["w_attn_o"], p["w_lru_o"], p["w_out"], p["norm_ffn_g"], p["w_ffn_gate"], p["w_ffn_up"],
      p["w_ffn_down"], p["ffn_conv_w"], p["ffn_conv_b"], fconv0, g_final)


def _block_diag(w):
    n, d, e = w.shape
    eye = jnp.eye(n, dtype=w.dtype)
    return (w[:, :, None, :] * eye[:, None, :, None]).reshape(n * d, n * e)


def _tri(n):
    j = lax.broadcasted_iota(jnp.int32, (n, n), 0)
    s = lax.broadcasted_iota(jnp.int32, (n, n), 1)
    return (j >= s).astype(BF16)


def _layer_params(l, norm_mix_g, w_in, b_gate, w_attn_o, w_lru_o, w_out, lru_conv_w, lru_conv_b,
                  w_lru_a, b_lru_a, w_lru_x, b_lru_x, lru_lambda, norm_ffn_g, w_ffn_gate, w_ffn_up,
                  ffn_conv_w, ffn_conv_b, w_ffn_down):
    row = lambda v: v[l].reshape(1, -1)
    return dict(
        norm_mix_g=row(norm_mix_g), w_in=w_in[l].astype(BF16), b_gate=row(b_gate),
        w_attn_o=w_attn_o[l].astype(BF16), w_lru_o=w_lru_o[l].astype(BF16),
        w_out=w_out[l].astype(BF16), lru_conv_w=lru_conv_w[l], lru_conv_b=row(lru_conv_b),
        w_lru_a=_block_diag(w_lru_a[l]).astype(BF16), b_lru_a=row(b_lru_a),
        w_lru_x=_block_diag(w_lru_x[l]).astype(BF16), b_lru_x=row(b_lru_x), lru_lambda=row(lru_lambda),
        norm_ffn_g=row(norm_ffn_g), w_ffn_gate=w_ffn_gate[l].astype(BF16), w_ffn_up=w_ffn_up[l].astype(BF16),
        ffn_conv_w=ffn_conv_w[l], ffn_conv_b=row(ffn_conv_b), w_ffn_down=w_ffn_down[l].astype(BF16))


def kernel(x_prompt, x_sample, cache_k, cache_v, page_table, state_lru_conv, state_lru_h, state_ffn_conv,
           norm_mix_g, w_in, b_gate, attn_logit_bias, w_attn_o, w_lru_o, w_out, lru_conv_w, lru_conv_b,
           w_lru_a, b_lru_a, w_lru_x, b_lru_x, lru_lambda, norm_ffn_g, w_ffn_gate, w_ffn_up,
           ffn_conv_w, ffn_conv_b, w_ffn_down, norm_final_g):
    depth = w_in.shape[0]
    bp, tp, d = x_prompt.shape
    bs, ts, _ = x_sample.shape
    width = state_lru_h.shape[-1]
    ffn = state_ffn_conv.shape[-1]
    n_pool = cache_k.shape[1]
    n_pages = page_table.shape[1]
    g_final = norm_final_g.reshape(1, d)

    tile_p = min(256, tp)
    tile_post = min(512, tp)
    seq_blk = min(32, bs)
    chunk = 512
    pages = next(c for c in (8, 4, 2, 1) if n_pages % (2 * c) == 0 and n_pages // c >= 4)
    tri_p = _tri(tile_p)
    tri_s = _tri(PAGE_SIZE)

    hp, hs = x_prompt, x_sample
    outs = [[] for _ in range(10)]
    for l in range(depth):
        p = _layer_params(l, norm_mix_g, w_in, b_gate, w_attn_o, w_lru_o, w_out, lru_conv_w, lru_conv_b,
                          w_lru_a, b_lru_a, w_lru_x, b_lru_x, lru_lambda, norm_ffn_g, w_ffn_gate, w_ffn_up,
                          ffn_conv_w, ffn_conv_b, w_ffn_down)
        bias = attn_logit_bias[l]
        last = l == depth - 1

        ktp, vtp, q, kt, v, gates_p, olru_p, lcp, lhp = _pre_call(
            hp, jnp.zeros((bp, lru_conv_w.shape[1] - 1, width), F32), jnp.zeros((bp, width), F32), p,
            bb=1, tt=tile_p, prompt_layout=True)
        ks, vs, qs, gates_s, olru_s, lcs, lhs = _pre_call(
            hs, state_lru_conv[l], state_lru_h[l], p, bb=seq_blk, tt=ts, prompt_layout=False)
        ckt = jnp.transpose(cache_k[l], (0, 2, 3, 1)).reshape(n_pool, ATTN_WIDTH, PAGE_SIZE)
        cvt = jnp.transpose(cache_v[l], (0, 2, 3, 1)).reshape(n_pool, ATTN_WIDTH, PAGE_SIZE)
        oa_p, oa_s = _attn_call(q, kt, v, tri_p, qs, ks, vs, tri_s, ckt, cvt, page_table, bias,
                                tq=tile_p, t=ts, pages=pages)
        hp, fcp = _post_call(hp, oa_p, olru_p, gates_p, jnp.zeros((bp, ffn_conv_w.shape[1] - 1, ffn), F32), p,
                             g_final, bb=1, tt=tile_post, chunk=chunk, final_norm=last)
        hs, fcs = _post_call(hs, oa_s, olru_s, gates_s, state_ffn_conv[l], p, g_final,
                             bb=seq_blk, tt=ts, chunk=chunk, final_norm=last)

        from_t = lambda m: jnp.transpose(m.reshape(N_HEADS, HEAD_DIM, bp, tp), (2, 3, 0, 1))
        layer = (from_t(ktp), ks.reshape(bs, ts, N_HEADS, HEAD_DIM),
                 from_t(vtp), vs.reshape(bs, ts, N_HEADS, HEAD_DIM),
                 lcp, lcs, lhp, lhs, fcp, fcs)
        for acc, val in zip(outs, layer):
            acc.append(val)
    return (hp, hs) + tuple(jnp.stack(o) for o in outs)
```

```python
import functools

import jax
import jax.numpy as jnp
from jax import lax
from jax.experimental import pallas as pl
from jax.experimental.pallas import tpu as pltpu

N_HEADS = 8
HEAD_DIM = 64
ATTN_WIDTH = N_HEADS * HEAD_DIM
LRU_C = 8.0
RMS_EPS = 1e-6
PAGE_SIZE = 128

LOG2E = 1.4426950408889634

F32 = jnp.float32
BF16 = jnp.bfloat16

VMEM_LIMIT_BYTES = 56 * 1024 * 1024
SUBLANES = 8
LANES = 128
HEADS_PER_TILE = LANES // HEAD_DIM
HEAD_TILES = ATTN_WIDTH // LANES
SHORT_TRIP = HEAD_TILES


def _const_spec(shape):
    zeros = (0,) * len(shape)
    return pl.BlockSpec(shape, lambda *_: zeros, pipeline_mode=pl.Buffered(1))


def _log_sigmoid(x):
    return jnp.minimum(x, 0.0) - jnp.log(1.0 + jnp.exp2(jnp.abs(x) * (-LOG2E)))


def _log2_sigmoid(x2):
    neg_abs = pltpu.bitcast(pltpu.bitcast(x2, jnp.uint32) | jnp.uint32(0x80000000), F32)
    return jnp.minimum(x2, 0.0) - jnp.log2(1.0 + jnp.exp2(neg_abs))


def _rms(x, g):
    ms = jnp.mean(x * x, axis=-1, keepdims=True)
    return (x * lax.rsqrt(ms + RMS_EPS)) * g


def _pre_kernel(x_ref, g_ref, win_ref, bgate_ref, cw_ref, cb_ref, wa_ref, ba_ref, wx_ref, bx_ref, lam_ref,
                conv0_ref, h0_ref, *refs, prompt_layout):
    qkv_refs = refs[:-6]
    gates_ref, olru_ref, convn_ref, hlast_ref, xpad_ref, hcarry_ref = refs[-6:]
    ti = pl.program_id(1)
    bb, tt, d = x_ref.shape
    rows = bb * tt
    width = xpad_ref.shape[-1]
    hist = cw_ref.shape[0] - 1

    @pl.when(ti == 0)
    def _():
        xpad_ref[:, SUBLANES - hist:SUBLANES, :] = conv0_ref[...]
        hcarry_ref[...] = h0_ref[...]

    x = x_ref[...].reshape(rows, d)
    xn = _rms(x, g_ref[...])
    proj = jnp.dot(xn.astype(BF16), win_ref[...], preferred_element_type=F32)

    a = ATTN_WIDTH
    qf = proj[:, 0:a] * -(HEAD_DIM ** -0.5 * LOG2E)
    kf = proj[:, a:2 * a]
    vf = proj[:, 2 * a:3 * a]
    if prompt_layout:
        ktf_ref, vtf_ref, q_ref, kt_ref, v_ref = qkv_refs
        kt = kf.T
        ktf_ref[...] = kt
        vtf_ref[...] = vf.T
        kt_ref[0] = kt.astype(BF16).reshape(HEAD_TILES, LANES, rows)
        for g in range(HEAD_TILES):
            q_ref[g] = qf[:, g * LANES:(g + 1) * LANES].astype(BF16)
            v_ref[g] = vf[:, g * LANES:(g + 1) * LANES].astype(BF16)
    else:
        k_ref, v_ref, q_ref = qkv_refs
        k_ref[...] = kf
        v_ref[...] = vf
        q_ref[...] = qf

    gates_ref[...] = jax.nn.sigmoid(proj[:, 3 * a + width:] + bgate_ref[...])

    xl = proj[:, 3 * a:3 * a + width].reshape(bb, tt, width)
    xpad_ref[:, SUBLANES:SUBLANES + tt, :] = xl
    xc = cb_ref[...].reshape(1, 1, width)
    for j in range(hist + 1):
        xc = xc + xpad_ref[:, SUBLANES - hist + j:SUBLANES - hist + j + tt, :] * cw_ref[j:j + 1, :].reshape(1, 1, width)
    tail = xpad_ref[:, SUBLANES - hist + tt:SUBLANES + tt, :]
    convn_ref[...] = tail
    xpad_ref[:, SUBLANES - hist:SUBLANES, :] = tail

    xc2 = xc.reshape(rows, width)
    xcb = xc2.astype(BF16)
    r = jax.nn.sigmoid(jnp.dot(xcb, wa_ref[...], preferred_element_type=F32) + ba_ref[...])
    i = jax.nn.sigmoid(jnp.dot(xcb, wx_ref[...], preferred_element_type=F32) + bx_ref[...])
    log_a = LRU_C * r * _log_sigmoid(lam_ref[...])
    av = jnp.exp(log_a)
    bv = jnp.sqrt(1.0 - av * av) * (i * xc2)

    A = av.reshape(bb, tt, width)
    B = bv.reshape(bb, tt, width)
    tidx = lax.broadcasted_iota(jnp.int32, (bb, tt, width), 1)
    s = 1
    while s < tt:
        a_sh = pltpu.roll(A, s, axis=1)
        b_sh = pltpu.roll(B, s, axis=1)
        valid = tidx >= s
        B = jnp.where(valid, A * b_sh + B, B)
        A = jnp.where(valid, A * a_sh, A)
        s *= 2
    hs = A * hcarry_ref[...].reshape(bb, 1, width) + B
    olru_ref[...] = hs.reshape(rows, width).astype(olru_ref.dtype)
    h_last = hs[:, tt - 1, :]
    hcarry_ref[...] = h_last
    hlast_ref[...] = h_last


def _pre_call(x, conv0, h0, p, *, bb, tt, prompt_layout):
    b, t, d = x.shape
    nb, nt = b // bb, t // tt
    rows = bb * tt
    n = b * t
    width = conv0.shape[-1]
    hist = conv0.shape[1]
    in_width = p["w_in"].shape[1]
    gate_w = in_width - 3 * ATTN_WIDTH - width
    a = ATTN_WIDTH

    def row_idx(bi, ti):
        return (bi * nt + ti, 0)

    def col_idx(bi, ti):
        return (0, bi * nt + ti)

    def tile_idx(bi, ti):
        return (0, bi * nt + ti, 0)

    in_specs = [
        pl.BlockSpec((bb, tt, d), lambda bi, ti: (bi, ti, 0)),
        _const_spec((1, d)),
        _const_spec((d, in_width)),
        _const_spec((1, gate_w)),
        _const_spec((hist + 1, width)),
        _const_spec((1, width)),
        _const_spec((width, width)),
        _const_spec((1, width)),
        _const_spec((width, width)),
        _const_spec((1, width)),
        _const_spec((1, width)),
        pl.BlockSpec((bb, hist, width), lambda bi, ti: (bi, 0, 0)),
        pl.BlockSpec((bb, width), lambda bi, ti: (bi, 0)),
    ]
    if prompt_layout:
        assert b == 1 and bb == 1
        qkv_shape = [
            jax.ShapeDtypeStruct((a, n), F32),
            jax.ShapeDtypeStruct((a, n), F32),
            jax.ShapeDtypeStruct((HEAD_TILES, n, LANES), BF16),
            jax.ShapeDtypeStruct((nt, HEAD_TILES, LANES, tt), BF16),
            jax.ShapeDtypeStruct((HEAD_TILES, n, LANES), BF16),
        ]
        qkv_specs = [
            pl.BlockSpec((a, rows), col_idx),
            pl.BlockSpec((a, rows), col_idx),
            pl.BlockSpec((HEAD_TILES, rows, LANES), tile_idx),
            pl.BlockSpec((1, HEAD_TILES, LANES, tt), lambda bi, ti: (ti, 0, 0, 0)),
            pl.BlockSpec((HEAD_TILES, rows, LANES), tile_idx),
        ]
    else:
        qkv_shape = [jax.ShapeDtypeStruct((n, a), F32)] * 3
        qkv_specs = [pl.BlockSpec((rows, a), row_idx)] * 3
    out_shape = qkv_shape + [
        jax.ShapeDtypeStruct((n, gate_w), F32),
        jax.ShapeDtypeStruct((n, width), BF16),
        jax.ShapeDtypeStruct((b, hist, width), F32),
        jax.ShapeDtypeStruct((b, width), F32),
    ]
    out_specs = qkv_specs + [
        pl.BlockSpec((rows, gate_w), row_idx),
        pl.BlockSpec((rows, width), row_idx),
        pl.BlockSpec((bb, hist, width), lambda bi, ti: (bi, 0, 0)),
        pl.BlockSpec((bb, width), lambda bi, ti: (bi, 0)),
    ]
    return pl.pallas_call(
        functools.partial(_pre_kernel, prompt_layout=prompt_layout),
        grid=(nb, nt),
        in_specs=in_specs,
        out_specs=out_specs,
        out_shape=out_shape,
        scratch_shapes=[pltpu.VMEM((bb, SUBLANES + tt, width), F32), pltpu.VMEM((bb, width), F32)],
        compiler_params=pltpu.CompilerParams(dimension_semantics=("arbitrary", "arbitrary"),
                                             vmem_limit_bytes=VMEM_LIMIT_BYTES),
        name="pre",
    )(x, p["norm_mix_g"], p["w_in"], p["b_gate"], p["lru_conv_w"], p["lru_conv_b"], p["w_lru_a"], p["b_lru_a"],
      p["w_lru_x"], p["b_lru_x"], p["lru_lambda"], conv0, h0)


def _sb_tile(nz, tri, carry, mask):
    lk = _log2_sigmoid(nz)
    if mask is not None:
        lk = jnp.where(mask, lk, 0.0)
    cin = jnp.dot(lk.astype(BF16), tri, preferred_element_type=F32) + carry
    w = jnp.exp2(cin - nz)
    if mask is not None:
        w = jnp.where(mask, w, 0.0)
    return w, cin[:, 0:1]


def _attn_prompt_tile(bias_ref, q_ref, kt_ref, v_ref, tri_ref, o_ref,
                      qs_ref, nz_buf, lk_buf, cin_buf, acc_ref, carry_ref, *, interleave):
    iq = pl.program_id(0)
    n_tiles, tq, _ = q_ref.shape
    tk = tri_ref.shape[0]
    rows = HEADS_PER_TILE * tq
    tri = tri_ref[...]
    first = lax.broadcasted_iota(jnp.int32, (tq, LANES), 1) < HEAD_DIM

    def neg_logits(g, j):
        nz = jnp.dot(qs_ref[g], kt_ref[j, g], preferred_element_type=F32)
        return jnp.concatenate([nz[e * tq:(e + 1) * tq] - bias_ref[HEADS_PER_TILE * g + e] * LOG2E
                                for e in range(HEADS_PER_TILE)], axis=0)

    def values(g, j):
        return v_ref[g, pl.ds(pl.multiple_of(j * tk, tk), tk), :]

    row = lax.broadcasted_iota(jnp.int32, (HEADS_PER_TILE, tq, tk), 1).reshape(rows, tk)
    col = lax.broadcasted_iota(jnp.int32, (rows, tk), 1)
    causal = col < row
    for g in range(n_tiles):
        qp = q_ref[g]
        zero = jnp.zeros_like(qp)
        qs_ref[g] = jnp.concatenate([jnp.where(first, qp, zero), jnp.where(first, zero, qp)], axis=0)
        w, carry = _sb_tile(neg_logits(g, iq), tri, jnp.zeros((rows, 1), F32), causal)
        acc_ref[g] = jnp.dot(w.astype(BF16), values(g, iq), preferred_element_type=F32)
        carry_ref[g] = carry

    def advance(u):
        g, jj = u
        wrap = jj + 1 >= iq
        return (jnp.where(wrap, g + 1, g), jnp.where(wrap, 0, jj + 1))

    def addr(u):
        g, jj = u
        return jnp.minimum(g, n_tiles - 1), jnp.maximum(iq - 1 - jj, 0)

    def iteration(slot, units, stages):
        ua, uc = units[0], units[2]
        ga, ja = addr(ua)
        gc, jc = addr(uc)
        nz = neg_logits(ga, ja)
        if "tri" in stages:
            cin_buf[1 - slot] = jnp.dot(lk_buf[1 - slot], tri, preferred_element_type=F32)
        if "w" in stages:
            carry = carry_ref[gc]
            cin = cin_buf[slot] + carry
            w = jnp.exp2(cin - nz_buf[slot])
            acc_ref[gc] = acc_ref[gc] + jnp.dot(w.astype(BF16), values(gc, jc), preferred_element_type=F32)
            carry_ref[gc] = cin[:, 0:1]
        nz_buf[slot] = nz
        lk_buf[slot] = _log2_sigmoid(nz).astype(BF16)

    def shift(units):
        return (advance(units[0]), units[0], units[1])

    def trip(_, units):
        chain = [units]
        for _k in range(SHORT_TRIP):
            chain.append(shift(chain[-1]))

        interleave([functools.partial(iteration, k % 2, chain[k], ("tri", "w")) for k in range(SHORT_TRIP)])
        return chain[-1]

    @pl.when(iq > 0)
    def _():
        zero = jnp.int32(0)
        units = ((zero, zero),) * 3
        iteration(0, units, ())
        units = shift(units)
        iteration(1, units, ("tri",))
        units = shift(units)
        lax.fori_loop(0, (n_tiles * iq) // SHORT_TRIP, trip, units)

    for g in range(n_tiles):
        acc = acc_ref[g]
        o_ref[g] = jnp.where(first, acc[0:tq], acc[tq:rows]).astype(o_ref.dtype)


K_PAGES, V_PAGES = 0, 1


def _sample_attention(pt_ref, bias_ref, qkv_hbm, tri_ref, ck_hbm, cv_hbm, o_ref,
                      kbuf, vbuf, sems, qkv_buf, qkv_sems, cnt_ref, qbd_ref, bcol_ref, kpad_ref, vpad_ref,
                      nz_buf, lk_buf, cin_buf, nzw_buf, acc_ref, carry_ref, *, pages):
    _, t, a = qkv_buf.shape
    n_seq = o_ref.shape[0] // t
    rows = N_HEADS * t
    n_pages = pt_ref.shape[1]
    n_groups = n_pages // pages
    tri = tri_ref[...]
    nt_dims = (((1,), (1,)), ((), ()))
    lane = lax.broadcasted_iota(jnp.int32, (t, a), 1)

    def qkv_copies(seq):
        rows_of_seq = pl.ds(pl.multiple_of(seq * t, t), t)
        return [pltpu.make_async_copy(hbm.at[rows_of_seq], qkv_buf.at[i], qkv_sems.at[i])
                for i, hbm in enumerate(qkv_hbm)]

    def own_lanes(h):
        return (lane >= h * HEAD_DIM) & (lane < (h + 1) * HEAD_DIM)

    def page_copies(kind, seq, grp, slot):
        hbm, buf = (ck_hbm, kbuf) if kind == K_PAGES else (cv_hbm, vbuf)
        return [pltpu.make_async_copy(hbm.at[pt_ref[seq, n_pages - 1 - (grp * pages + j)]], buf.at[slot, j],
                                      sems.at[kind, slot, j]) for j in range(pages)]

    def start(kind, seq, grp, slot):
        for copy in page_copies(kind, seq, grp, slot):
            copy.start()

    def wait(kind, seq, grp, slot):
        for copy in page_copies(kind, seq, grp, slot):
            copy.wait()

    def weights(nz, cin0, carry, mask):
        cin = cin0 + carry
        w = jnp.exp2(cin - nz)
        if mask is not None:
            w = jnp.where(mask, w, 0.0)
        return w, cin[:, 0:1]

    page_cols = [slice(j * PAGE_SIZE, (j + 1) * PAGE_SIZE) for j in range(pages)]

    def stage_z(slot):
        nzs = [jnp.dot(qbd_ref[...], kbuf[slot, j], preferred_element_type=F32) for j in range(pages)]
        for nz, cols in zip(nzs, page_cols):
            nz = nz - bcol_ref[...]
            nz_buf[slot, :, cols] = nz
            lk_buf[slot, :, cols] = _log2_sigmoid(nz).astype(BF16)

    def stage_tri(slot):
        for cols in page_cols:
            cin_buf[slot, :, cols] = jnp.dot(lk_buf[slot, :, cols], tri, preferred_element_type=F32)
        nzw_buf[slot] = nz_buf[slot]

    def stage_w(slot):
        acc = acc_ref[...]
        carry = carry_ref[...]
        for j, cols in enumerate(page_cols):
            w, carry = weights(nzw_buf[slot, :, cols], cin_buf[slot, :, cols], carry, None)
            acc = acc + lax.dot_general(w, vbuf[slot, j], nt_dims, preferred_element_type=F32)
        acc_ref[...] = acc
        carry_ref[...] = carry

    def setup(seq):
        for copy in qkv_copies(seq):
            copy.wait()
        q = qkv_buf[0]
        for h in range(N_HEADS):
            qbd_ref[h * t:(h + 1) * t, :] = jnp.where(own_lanes(h), q, 0.0)
            bcol_ref[h * t:(h + 1) * t, :] = jnp.full((t, PAGE_SIZE), bias_ref[h] * LOG2E, F32)
        kpad_ref[...] = jnp.zeros_like(kpad_ref)
        vpad_ref[...] = jnp.zeros_like(vpad_ref)
        kpad_ref[0:t, :] = qkv_buf[1]
        vpad_ref[0:t, :] = qkv_buf[2]
        start_for_next_sequence(seq, lambda nxt: [copy.start() for copy in qkv_copies(nxt)])
        qpos = lax.broadcasted_iota(jnp.int32, (N_HEADS, t, PAGE_SIZE), 1).reshape(rows, PAGE_SIZE)
        kpos = lax.broadcasted_iota(jnp.int32, (rows, PAGE_SIZE), 1)
        causal = kpos < qpos
        nz = lax.dot_general(qbd_ref[...], kpad_ref[...], nt_dims, preferred_element_type=F32) - bcol_ref[...]
        lk = jnp.where(causal, _log2_sigmoid(nz), 0.0)
        w, carry = weights(nz, jnp.dot(lk.astype(BF16), tri, preferred_element_type=F32),
                           jnp.zeros((rows, 1), F32), causal)
        acc_ref[...] = jnp.dot(w, vpad_ref[...], preferred_element_type=F32)
        carry_ref[...] = carry

    def finish(seq):
        acc = acc_ref[...]
        out = jnp.zeros((t, a), F32)
        for h in range(N_HEADS):
            out = jnp.where(own_lanes(h), acc[h * t:(h + 1) * t, :], out)
        o_ref[pl.ds(pl.multiple_of(seq * t, t), t), :] = out

    def iteration(seq, s, slot, stages, between=()):
        others = iter(between)
        if "next_k" in stages:
            start(K_PAGES, seq, s + 1, 1 - slot)
        if "next_v" in stages:
            start(V_PAGES, seq, s - 1, 1 - slot)
        if "z" in stages:
            wait(K_PAGES, seq, s, slot)
        if "w" in stages:
            wait(V_PAGES, seq, s - 2, slot)
        if "z" in stages:
            stage_z(slot)
            next(others, lambda: None)()
        if "tri" in stages:
            stage_tri(1 - slot)
            next(others, lambda: None)()
        if "w" in stages:
            stage_w(slot)
        for other in others:
            other()

    def start_for_next_sequence(seq, start_fn):
        @pl.when(seq + 1 < n_seq)
        def _():
            start_fn(seq + 1)

    full = ("next_k", "next_v", "z", "tri", "w")
    last = n_groups + 1

    def run(seq, s, active, also):
        def case(cond, fn):
            pl.when(active & cond)(fn)

        def first():
            setup(seq)
            iteration(seq, 0, 0, ("z",), also)

        def drain_a():
            start_for_next_sequence(seq, lambda nxt: start(K_PAGES, nxt, 0, 0))
            iteration(seq, n_groups - 1, 1, ("next_v", "z", "tri", "w"), also)

        def drain_b():
            start_for_next_sequence(seq, lambda nxt: start(K_PAGES, nxt, 1, 1))
            iteration(seq, n_groups, 0, ("next_v", "tri", "w"), also)

        def drain_c():
            start_for_next_sequence(seq, lambda nxt: start(V_PAGES, nxt, 0, 0))
            iteration(seq, last, 1, ("w",), also)
            finish(seq)

        steady = (s >= 2) & (s <= n_groups - 2)
        case(s == 0, first)
        case(s == 1, lambda: iteration(seq, 1, 1, ("next_k", "z", "tri"), also))
        case(steady & (s % 2 == 0), lambda: iteration(seq, s, 0, full, also))
        case(steady & (s % 2 == 1), lambda: iteration(seq, s, 1, full, also))
        case(s == n_groups - 1, drain_a)
        case(s == n_groups, drain_b)
        case(s == last, drain_c)

    def init():
        cnt_ref[0] = 0
        cnt_ref[1] = 0
        zero = jnp.int32(0)
        start(K_PAGES, zero, 0, 0)
        start(K_PAGES, zero, 1, 1)
        start(V_PAGES, zero, 0, 0)
        for copy in qkv_copies(zero):
            copy.start()

    def tick(also=()):
        seq, s = cnt_ref[0], cnt_ref[1]
        active = seq < n_seq
        run(seq, s, active, also)

        @pl.when(jnp.logical_not(active))
        def _():
            for other in also:
                other()

        @pl.when(active)
        def _():
            done = s == last
            cnt_ref[0] = jnp.where(done, seq + 1, seq)
            cnt_ref[1] = jnp.where(done, 0, s + 1)

    def drain():
        remaining = (n_seq - cnt_ref[0]) * (last + 1) - cnt_ref[1]
        lax.fori_loop(0, remaining, lambda _, c: (tick(), c)[1], 0)

    return init, tick, drain


N_PROMPT_SCRATCH = 6


def _attn_kernel(pt_ref, bias_ref, q_ref, kt_ref, v_ref, tri_ref, qs_hbm, ks_hbm, vs_hbm, tris_ref, ck_hbm, cv_hbm,
                 o_ref, os_ref, *scratch, pages):
    init, tick, drain = _sample_attention(pt_ref, bias_ref, (qs_hbm, ks_hbm, vs_hbm), tris_ref, ck_hbm, cv_hbm,
                                          os_ref, *scratch[N_PROMPT_SCRATCH:], pages=pages)
    iq = pl.program_id(0)
    pl.when(iq == 0)(init)
    _attn_prompt_tile(bias_ref, q_ref, kt_ref, v_ref, tri_ref, o_ref, *scratch[:N_PROMPT_SCRATCH],
                      interleave=tick)
    pl.when(iq == pl.num_programs(0) - 1)(drain)


def _attn_call(q, kt, v, tri, qs, ks, vs, tri_s, cache_kt, cache_vt, page_table, bias, *, tq, t, pages):
    n_tiles, tp, _ = q.shape
    _, _, _, tk = kt.shape
    assert tq == tk and n_tiles % SHORT_TRIP == 0 and SHORT_TRIP % 2 == 0
    rows_p = HEADS_PER_TILE * tq
    n, a = qs.shape
    n_pages = page_table.shape[1]
    assert n_pages % (2 * pages) == 0 and n_pages // pages >= 4
    rows_s = N_HEADS * t
    group = pages * PAGE_SIZE

    def const(shape):
        zeros = (0,) * len(shape)
        return pl.BlockSpec(shape, lambda i, pt: zeros, pipeline_mode=pl.Buffered(1))

    hbm = pl.BlockSpec(memory_space=pl.ANY)
    grid_spec = pltpu.PrefetchScalarGridSpec(
        num_scalar_prefetch=1,
        grid=(tp // tq,),
        in_specs=[pl.BlockSpec(memory_space=pltpu.SMEM),
                  pl.BlockSpec((n_tiles, tq, LANES), lambda i, pt: (0, i, 0)),
                  const(kt.shape), const(v.shape), const(tri.shape),
                  hbm, hbm, hbm, const(tri_s.shape), hbm, hbm],
        out_specs=[pl.BlockSpec((n_tiles, tq, LANES), lambda i, pt: (0, i, 0)),
                   pl.BlockSpec((n, a), lambda i, pt: (0, 0))],
        scratch_shapes=[
            pltpu.VMEM((n_tiles, rows_p, LANES), BF16),
            pltpu.VMEM((2, rows_p, tk), F32),
            pltpu.VMEM((2, rows_p, tk), BF16),
            pltpu.VMEM((2, rows_p, tk), F32),
            pltpu.VMEM((n_tiles, rows_p, LANES), F32),
            pltpu.VMEM((n_tiles, rows_p, 1), F32),
            pltpu.VMEM((2, pages, a, PAGE_SIZE), F32),
            pltpu.VMEM((2, pages, a, PAGE_SIZE), F32),
            pltpu.SemaphoreType.DMA((2, 2, pages)),
            pltpu.VMEM((3, t, a), F32),
            pltpu.SemaphoreType.DMA((3,)),
            pltpu.SMEM((2,), jnp.int32),
            pltpu.VMEM((rows_s, a), F32),
            pltpu.VMEM((rows_s, PAGE_SIZE), F32),
            pltpu.VMEM((PAGE_SIZE, a), F32),
            pltpu.VMEM((PAGE_SIZE, a), F32),
            pltpu.VMEM((2, rows_s, group), F32),
            pltpu.VMEM((2, rows_s, group), BF16),
            pltpu.VMEM((2, rows_s, group), F32),
            pltpu.VMEM((2, rows_s, group), F32),
            pltpu.VMEM((rows_s, a), F32),
            pltpu.VMEM((rows_s, 1), F32),
        ],
    )
    return pl.pallas_call(
        functools.partial(_attn_kernel, pages=pages),
        grid_spec=grid_spec,
        out_shape=[jax.ShapeDtypeStruct((n_tiles, tp, LANES), BF16), jax.ShapeDtypeStruct((n, a), F32)],
        compiler_params=pltpu.CompilerParams(dimension_semantics=("arbitrary",),
                                             vmem_limit_bytes=VMEM_LIMIT_BYTES),
        name="attn",
    )(page_table, bias, q, kt, v, tri, qs, ks, vs, tri_s, cache_kt, cache_vt)


def _post_kernel(x_ref, oa_ref, ol_ref, gates_ref, wao_ref, wlo_ref, wout_ref, gffn_ref, wg_ref, wu_ref, wd_ref,
                 fcw_ref, fcb_ref, fconv0_ref, gfin_ref,
                 y_ref, fconvn_ref,
                 gpad_ref, *, chunk, final_norm):
    ti = pl.program_id(1)
    bb, tt, d = x_ref.shape
    rows = bb * tt
    ffn = wg_ref.shape[1]
    hist = fcw_ref.shape[0] - 1

    @pl.when(ti == 0)
    def _():
        gpad_ref[:, SUBLANES - hist:SUBLANES, :] = fconv0_ref[...]

    if len(oa_ref.shape) == 3:
        oa = jnp.concatenate([oa_ref[g] for g in range(oa_ref.shape[0])], axis=1)
    else:
        oa = oa_ref[...]
    ao = jnp.dot(oa.astype(BF16), wao_ref[...], preferred_element_type=F32)
    lo = jnp.dot(ol_ref[...], wlo_ref[...], preferred_element_type=F32)
    mixed = gates_ref[:, 0:d] * ao + gates_ref[:, d:2 * d] * lo
    hres = x_ref[...].reshape(rows, d) + jnp.dot(mixed.astype(BF16), wout_ref[...], preferred_element_type=F32)
    hn = _rms(hres, gffn_ref[...]).astype(BF16)

    acc = jnp.zeros((rows, d), F32)
    for c in range(ffn // chunk):
        cs = slice(c * chunk, (c + 1) * chunk)
        gp = jnp.dot(hn, wg_ref[:, cs], preferred_element_type=F32).reshape(bb, tt, chunk)
        gpad_ref[:, SUBLANES:SUBLANES + tt, cs] = gp
        gc = fcb_ref[:, cs].reshape(1, 1, chunk)
        for j in range(hist + 1):
            gc = gc + (gpad_ref[:, SUBLANES - hist + j:SUBLANES - hist + j + tt, cs]
                       * fcw_ref[j:j + 1, cs].reshape(1, 1, chunk))
        tail = gpad_ref[:, SUBLANES - hist + tt:SUBLANES + tt, cs]
        fconvn_ref[:, :, cs] = tail
        gpad_ref[:, SUBLANES - hist:SUBLANES, cs] = tail
        up = jnp.dot(hn, wu_ref[:, cs], preferred_element_type=F32)
        act = jax.nn.gelu(gc.reshape(rows, chunk), approximate=True) * up
        acc = acc + jnp.dot(act.astype(BF16), wd_ref[cs, :], preferred_element_type=F32)

    out = hres + acc
    if final_norm:
        out = _rms(out, gfin_ref[...])
    y_ref[...] = out.reshape(bb, tt, d)


def _post_call(x, oa, ol, gates, fconv0, p, g_final, *, bb, tt, chunk, final_norm):
    b, t, d = x.shape
    nb, nt = b // bb, t // tt
    rows = bb * tt
    ffn = p["w_ffn_gate"].shape[1]
    hist = fconv0.shape[1]
    width = ol.shape[1]

    def row_idx(bi, ti):
        return (bi * nt + ti, 0)

    if oa.ndim == 3:
        oa_spec = pl.BlockSpec((oa.shape[0], rows, oa.shape[2]), lambda bi, ti: (0, bi * nt + ti, 0))
    else:
        oa_spec = pl.BlockSpec((rows, oa.shape[1]), row_idx)
    in_specs = [
        pl.BlockSpec((bb, tt, d), lambda bi, ti: (bi, ti, 0)),
        oa_spec,
        pl.BlockSpec((rows, width), row_idx),
        pl.BlockSpec((rows, 2 * d), row_idx),
        _const_spec((ATTN_WIDTH, d)),
        _const_spec((width, d)),
        _const_spec((d, d)),
        _const_spec((1, d)),
        _const_spec((d, ffn)),
        _const_spec((d, ffn)),
        _const_spec((ffn, d)),
        _const_spec((hist + 1, ffn)),
        _const_spec((1, ffn)),
        pl.BlockSpec((bb, hist, ffn), lambda bi, ti: (bi, 0, 0)),
        _const_spec((1, d)),
    ]
    return pl.pallas_call(
        functools.partial(_post_kernel, chunk=chunk, final_norm=final_norm),
        grid=(nb, nt),
        in_specs=in_specs,
        out_specs=[pl.BlockSpec((bb, tt, d), lambda bi, ti: (bi, ti, 0)),
                   pl.BlockSpec((bb, hist, ffn), lambda bi, ti: (bi, 0, 0))],
        out_shape=[jax.ShapeDtypeStruct((b, t, d), F32), jax.ShapeDtypeStruct((b, hist, ffn), F32)],
        scratch_shapes=[pltpu.VMEM((bb, SUBLANES + tt, ffn), F32)],
        compiler_params=pltpu.CompilerParams(dimension_semantics=("arbitrary", "arbitrary"),
                                             vmem_limit_bytes=VMEM_LIMIT_BYTES),
        name="post",
    )(x, oa, ol, gates, p["w_attn_o"], p["w_lru_o"], p["w_out"], p["norm_ffn_g"], p["w_ffn_gate"], p["w_ffn_up"],
      p["w_ffn_down"], p["ffn_conv_w"], p["ffn_conv_b"], fconv0, g_final)


def _block_diag(w):
    n, d, e = w.shape
    eye = jnp.eye(n, dtype=w.dtype)
    return (w[:, :, None, :] * eye[:, None, :, None]).reshape(n * d, n * e)


def _tri(n):
    j = lax.broadcasted_iota(jnp.int32, (n, n), 0)
    s = lax.broadcasted_iota(jnp.int32, (n, n), 1)
    return (j >= s).astype(BF16)


def _layer_params(l, norm_mix_g, w_in, b_gate, w_attn_o, w_lru_o, w_out, lru_conv_w, lru_conv_b,
                  w_lru_a, b_lru_a, w_lru_x, b_lru_x, lru_lambda, norm_ffn_g, w_ffn_gate, w_ffn_up,
                  ffn_conv_w, ffn_conv_b, w_ffn_down):
    row = lambda v: v[l].reshape(1, -1)
    return dict(
        norm_mix_g=row(norm_mix_g), w_in=w_in[l].astype(BF16), b_gate=row(b_gate),
        w_attn_o=w_attn_o[l].astype(BF16), w_lru_o=w_lru_o[l].astype(BF16),
        w_out=w_out[l].astype(BF16), lru_conv_w=lru_conv_w[l], lru_conv_b=row(lru_conv_b),
        w_lru_a=_block_diag(w_lru_a[l]).astype(BF16), b_lru_a=row(b_lru_a),
        w_lru_x=_block_diag(w_lru_x[l]).astype(BF16), b_lru_x=row(b_lru_x), lru_lambda=row(lru_lambda),
        norm_ffn_g=row(norm_ffn_g), w_ffn_gate=w_ffn_gate[l].astype(BF16), w_ffn_up=w_ffn_up[l].astype(BF16),
        ffn_conv_w=ffn_conv_w[l], ffn_conv_b=row(ffn_conv_b), w_ffn_down=w_ffn_down[l].astype(BF16))


def kernel(x_prompt, x_sample, cache_k, cache_v, page_table, state_lru_conv, state_lru_h, state_ffn_conv,
           norm_mix_g, w_in, b_gate, attn_logit_bias, w_attn_o, w_lru_o, w_out, lru_conv_w, lru_conv_b,
           w_lru_a, b_lru_a, w_lru_x, b_lru_x, lru_lambda, norm_ffn_g, w_ffn_gate, w_ffn_up,
           ffn_conv_w, ffn_conv_b, w_ffn_down, norm_final_g):
    depth = w_in.shape[0]
    bp, tp, d = x_prompt.shape
    bs, ts, _ = x_sample.shape
    width = state_lru_h.shape[-1]
    ffn = state_ffn_conv.shape[-1]
    n_pool = cache_k.shape[1]
    n_pages = page_table.shape[1]
    g_final = norm_final_g.reshape(1, d)

    tile_p = min(256, tp)
    tile_post = min(512, tp)
    seq_blk = min(32, bs)
    chunk = 512
    pages = next(c for c in (8, 4, 2, 1) if n_pages % (2 * c) == 0 and n_pages // c >= 4)
    tri_p = _tri(tile_p)
    tri_s = _tri(PAGE_SIZE)

    hp, hs = x_prompt, x_sample
    outs = [[] for _ in range(10)]
    for l in range(depth):
        p = _layer_params(l, norm_mix_g, w_in, b_gate, w_attn_o, w_lru_o, w_out, lru_conv_w, lru_conv_b,
                          w_lru_a, b_lru_a, w_lru_x, b_lru_x, lru_lambda, norm_ffn_g, w_ffn_gate, w_ffn_up,
                          ffn_conv_w, ffn_conv_b, w_ffn_down)
        bias = attn_logit_bias[l]
        last = l == depth - 1

        ktp, vtp, q, kt, v, gates_p, olru_p, lcp, lhp = _pre_call(
            hp, jnp.zeros((bp, lru_conv_w.shape[1] - 1, width), F32), jnp.zeros((bp, width), F32), p,
            bb=1, tt=tile_p, prompt_layout=True)
        ks, vs, qs, gates_s, olru_s, lcs, lhs = _pre_call(
            hs, state_lru_conv[l], state_lru_h[l], p, bb=seq_blk, tt=ts, prompt_layout=False)
        ckt = jnp.transpose(cache_k[l], (0, 2, 3, 1)).reshape(n_pool, ATTN_WIDTH, PAGE_SIZE)
        cvt = jnp.transpose(cache_v[l], (0, 2, 3, 1)).reshape(n_pool, ATTN_WIDTH, PAGE_SIZE)
        oa_p, oa_s = _attn_call(q, kt, v, tri_p, qs, ks, vs, tri_s, ckt, cvt, page_table, bias,
                                tq=tile_p, t=ts, pages=pages)
        hp, fcp = _post_call(hp, oa_p, olru_p, gates_p, jnp.zeros((bp, ffn_conv_w.shape[1] - 1, ffn), F32), p,
                             g_final, bb=1, tt=tile_post, chunk=chunk, final_norm=last)
        hs, fcs = _post_call(hs, oa_s, olru_s, gates_s, state_ffn_conv[l], p, g_final,
                             bb=seq_blk, tt=ts, chunk=chunk, final_norm=last)

        from_t = lambda m: jnp.transpose(m.reshape(N_HEADS, HEAD_DIM, bp, tp), (2, 3, 0, 1))
        layer = (from_t(ktp), ks.reshape(bs, ts, N_HEADS, HEAD_DIM),
                 from_t(vtp), vs.reshape(bs, ts, N_HEADS, HEAD_DIM),
                 lcp, lcs, lhp, lhs, fcp, fcs)
        for acc, val in zip(outs, layer):
            acc.append(val)
    return (hp, hs) + tuple(jnp.stack(o) for o in outs)
```
